```python
import math
import jax, jax.numpy as jnp
from jax import lax
import numpy as np

D_MODEL = 1024
BATCH = 4
SEQ = 4096
DEPTH = 4
DEC_BATCH = 32
DEC_SEQ = 4
PAST_LEN = 8192
PAGE_SIZE = 128

N_BRANCH = 4
D_BR = D_MODEL // N_BRANCH
HEAD_DIM = 64
N_HEADS = D_BR // HEAD_DIM
CONV_W = 3
POOL_WINDOWS = (2, 4, 8, 16)
POOL_GROUP = D_BR // len(POOL_WINDOWS)
POOL_CTX = max(POOL_WINDOWS) - 1
RANK_W = D_MODEL // 16
RANK_A = D_MODEL // 16
RANK_G = D_MODEL // 8
D_CONV_IN = 3 * D_BR
D_RWKV_IN = 3 * D_BR + RANK_W + RANK_A + RANK_G
D_POOL_IN = D_BR
D_SB_IN = 3 * D_BR
D_IN = D_CONV_IN + D_RWKV_IN + D_POOL_IN + D_SB_IN
IN_SPLITS = (D_CONV_IN, D_CONV_IN + D_RWKV_IN, D_CONV_IN + D_RWKV_IN + D_POOL_IN)
RWKV_SPLITS = (D_BR, 2 * D_BR, 3 * D_BR, 3 * D_BR + RANK_W, 3 * D_BR + RANK_W + RANK_A)
D_FF = ((8 * D_MODEL // 3 + 127) // 128) * 128
N_MOD = 9
SB_BLOCK = 128
SB_BIAS_INIT = -8.0
EPS = 1e-6
LNX_EPS = 64e-5

kernel_name = 'hybrid_conv_rwkv7_pool_stickbreak_step'


def rmsnorm(x, g):
    xf = x.astype(jnp.float32)
    y = xf * lax.rsqrt(jnp.mean(xf * xf, axis=-1, keepdims=True) + EPS)
    return (y * g.astype(jnp.float32)).astype(x.dtype)


def heads(t):
    return t.reshape(t.shape[:-1] + (N_HEADS, HEAD_DIM)).astype(jnp.float32)


def swiglu(u, w_up, w_down):
    h = u @ w_up
    a, b = jnp.split(h, 2, axis=-1)
    return (jax.nn.silu(a) * b) @ w_down


def short_conv_mixer(p, conv_prev, conv_w):
    b_gate, c_gate, xv = jnp.split(p, 3, axis=-1)
    z = c_gate * xv
    z_ext = jnp.concatenate([conv_prev.astype(z.dtype), z], axis=1)
    T = z.shape[1]
    y = conv_w[0] * z_ext[:, 0:T]
    for i in range(1, CONV_W):
        y = y + conv_w[i] * z_ext[:, i:i + T]
    return b_gate * y, z_ext[:, -(CONV_W - 1):]


def rwkv7_mixer(p, shift_prev, wkv_prev, mu, w0, w_w2, a0, w_a2, w_g2, k_k, k_a, r_k, lnx_g, lnx_b):
    B, T, _ = p.shape
    p_prev = jnp.concatenate([shift_prev[:, None, :].astype(p.dtype), p[:, :-1]], axis=1)
    xs = p + (p_prev - p) * mu
    r, k, v, wd, ad, gd = jnp.split(xs, RWKV_SPLITS, axis=-1)
    w_log = -jax.nn.softplus(-(w0 + jnp.tanh(wd) @ w_w2)) - 0.5
    decay = jnp.exp(-jnp.exp(w_log.astype(jnp.float32)))
    a = jax.nn.sigmoid(a0 + ad @ w_a2)
    g = jax.nn.sigmoid(gd) @ w_g2
    kk = heads(k * k_k)
    kk = kk / jnp.maximum(jnp.sqrt(jnp.sum(kk * kk, axis=-1, keepdims=True)), 1e-12)
    k = k * (1.0 + (a - 1.0) * k_a)
    r_h, w_h, k_h, v_h, a_h = heads(r), heads(decay), heads(k), heads(v), heads(a)

    def step(S, inp):
        r_t, w_t, k_t, v_t, kk_t, a_t = inp
        sa = jnp.einsum('bhij,bhj->bhi', S, -kk_t)
        S = (S * w_t[:, :, None, :] + sa[..., None] * (kk_t * a_t)[:, :, None, :]
             + v_t[..., None] * k_t[:, :, None, :])
        return S, jnp.einsum('bhij,bhj->bhi', S, r_t)

    seq = tuple(jnp.moveaxis(t, 1, 0) for t in (r_h, w_h, k_h, v_h, kk, a_h))
    S_last, y = lax.scan(step, wkv_prev.astype(jnp.float32), seq)
    y = jnp.moveaxis(y, 0, 1)
    mean = jnp.mean(y, axis=-1, keepdims=True)
    var = jnp.mean(jnp.square(y - mean), axis=-1, keepdims=True)
    y = (y - mean) * lax.rsqrt(var + LNX_EPS) * heads(lnx_g) + heads(lnx_b)
    bonus = jnp.sum(r_h * k_h * heads(r_k), axis=-1, keepdims=True) * v_h
    out = (y + bonus).reshape(B, T, D_BR).astype(p.dtype) * g
    return out, p[:, -1], S_last


def pool_mixer(p, pool_prev, pos0, w_pool, pool_scale):
    B, T, _ = p.shape
    z_ext = jnp.concatenate([pool_prev.astype(p.dtype), p], axis=1).astype(jnp.float32)
    cs = jnp.concatenate([jnp.zeros((B, 1, D_BR), jnp.float32), jnp.cumsum(z_ext, axis=1)], axis=1)
    pos = pos0 + jnp.arange(T)
    outs = []
    for gi, w in enumerate(POOL_WINDOWS):
        sl = slice(gi * POOL_GROUP, (gi + 1) * POOL_GROUP)
        s = cs[:, POOL_CTX + 1:POOL_CTX + 1 + T, sl] - cs[:, POOL_CTX + 1 - w:POOL_CTX + 1 - w + T, sl]
        cnt = jnp.minimum(w, pos + 1).astype(jnp.float32)[None, :, None]
        outs.append(s / cnt - z_ext[:, POOL_CTX:, sl])
    d = jnp.stack(outs, axis=2).astype(p.dtype)
    y = jnp.einsum('btgc,gcd->btgd', d, w_pool).reshape(B, T, D_BR) * pool_scale
    return y, z_ext[:, -POOL_CTX:].astype(p.dtype)


def stick_breaking(q, k, v, qpos, kpos, sb_bias):
    z = jnp.einsum('bqhd,bkhd->bhqk', q.astype(jnp.float32), k.astype(jnp.float32)) * (HEAD_DIM ** -0.5)
    z = z + sb_bias.astype(jnp.float32)[None, :, None, None]
    causal = kpos[None, :] < qpos[:, None]
    log_keep = jnp.where(causal, jax.nn.log_sigmoid(-z), 0.0)
    after = lax.cumsum(log_keep, axis=3, reverse=True) - log_keep
    A = jnp.where(causal, jnp.exp(jax.nn.log_sigmoid(z) + after), 0.0)
    return jnp.einsum('bhqk,bkhd->bqhd', A, v.astype(jnp.float32)).astype(v.dtype)


def sb_attention(q, k_all, v_all, pos0, sb_bias):
    B, T, H, d = q.shape
    kpos = jnp.arange(k_all.shape[1])
    qpos = pos0 + jnp.arange(T)
    blk = SB_BLOCK if T % SB_BLOCK == 0 else T
    nb = T // blk
    qb = jnp.moveaxis(q.reshape(B, nb, blk, H, d), 1, 0)
    pb = qpos.reshape(nb, blk)
    o = lax.map(lambda args: stick_breaking(args[0], k_all, v_all, args[1], kpos, sb_bias), (qb, pb))
    return jnp.moveaxis(o, 0, 1).reshape(B, T, H * d)


def trunk_layer(l, x, c, pos0, k_past, v_past, conv_prev, shift_prev, wkv_prev, pool_prev, P):
    B, T, D = x.shape
    mod = (jax.nn.silu(c) @ P['ada_w'][l] + P['ada_b'][l]).reshape(B, N_MOD, D)
    sh1, sc1, g1, sh2, sc2, g2, sh3, sc3, g3 = [mod[:, i][:, None, :] for i in range(N_MOD)]
    ng = P['norm_g'][l]
    u = rmsnorm(x, ng[0]) * (1.0 + sc1) + sh1
    x = x + 0.5 * g1 * swiglu(u, P['w_ffn_up'][l, 0], P['w_ffn_down'][l, 0])
    u = rmsnorm(x, ng[1]) * (1.0 + sc2) + sh2
    proj = u @ P['w_in'][l]
    p_conv, p_rwkv, p_pool, p_sb = jnp.split(proj, IN_SPLITS, axis=-1)
    y_a, conv_new = short_conv_mixer(p_conv, conv_prev, P['conv_w'][l])
    y_b, shift_new, wkv_new = rwkv7_mixer(
        p_rwkv, shift_prev, wkv_prev, P['rwkv_mu'][l], P['rwkv_w0'][l], P['rwkv_w2'][l],
        P['rwkv_a0'][l], P['rwkv_a2'][l], P['rwkv_g2'][l], P['rwkv_k_k'][l], P['rwkv_k_a'][l],
        P['rwkv_r_k'][l], P['rwkv_lnx_g'][l], P['rwkv_lnx_b'][l])
    y_c, pool_new = pool_mixer(p_pool, pool_prev, pos0, P['pool_w'][l], P['pool_scale'][l])
    q, k, v = jnp.split(p_sb, 3, axis=-1)
    q = rmsnorm(q.reshape(B, T, N_HEADS, HEAD_DIM), P['q_norm_g'][l])
    k = rmsnorm(k.reshape(B, T, N_HEADS, HEAD_DIM), P['k_norm_g'][l])
    v = v.reshape(B, T, N_HEADS, HEAD_DIM)
    k_all = jnp.concatenate([k_past.astype(k.dtype), k], axis=1)
    v_all = jnp.concatenate([v_past.astype(v.dtype), v], axis=1)
    y_d = sb_attention(q, k_all, v_all, pos0, P['sb_bias'][l])
    merged = jnp.zeros_like(x)
    for i, y_i in enumerate((y_a, y_b, y_c, y_d)):
        merged = merged + jax.nn.sigmoid(u @ P['w_gate'][l, i]) * (y_i @ P['w_branch'][l, i])
    x = x + g2 * (merged @ P['w_out'][l])
    u = rmsnorm(x, ng[2]) * (1.0 + sc3) + sh3
    x = x + 0.5 * g3 * swiglu(u, P['w_ffn_up'][l, 1], P['w_ffn_down'][l, 1])
    return x, (k, v, conv_new, shift_new, wkv_new, pool_new)


def setup_inputs(seed: int = 0) -> dict:
    key = jax.random.key(seed)
    keys = list(jax.random.split(key, 48))
    f32 = jnp.float32
    L = DEPTH

    def nrm(shape, scale):
        return jax.random.normal(keys.pop(), shape, f32) * scale

    def near(shape, base):
        return base + nrm(shape, 0.02)

    n_pages = PAST_LEN // PAGE_SIZE
    n_used = DEC_BATCH * n_pages
    n_phys = (n_used * 5 + 3) // 4
    page_table = jax.random.permutation(keys.pop(), n_phys)[:n_used].reshape(DEC_BATCH, n_pages).astype(jnp.int32)
    return {
        'x_prompt': nrm((BATCH, SEQ, D_MODEL), 1.0),
        'x_sample': nrm((DEC_BATCH, DEC_SEQ, D_MODEL), 1.0),
        'c_prompt': nrm((BATCH, D_MODEL), 1.0),
        'c_sample': nrm((DEC_BATCH, D_MODEL), 1.0),
        'cache_k': nrm((L, n_phys, PAGE_SIZE, N_HEADS, HEAD_DIM), 1.0),
        'cache_v': nrm((L, n_phys, PAGE_SIZE, N_HEADS, HEAD_DIM), 1.0),
        'page_table': page_table,
        'state_conv': nrm((L, DEC_BATCH, CONV_W - 1, D_BR), 0.5),
        'state_shift': nrm((L, DEC_BATCH, D_RWKV_IN), 0.5),
        'state_wkv': nrm((L, DEC_BATCH, N_HEADS, HEAD_DIM, HEAD_DIM), 0.5),
        'state_pool': nrm((L, DEC_BATCH, POOL_CTX, D_BR), 0.5),
        'ada_w': nrm((L, D_MODEL, N_MOD * D_MODEL), 0.5 * D_MODEL ** -0.5),
        'ada_b': nrm((L, N_MOD * D_MODEL), 0.02),
        'norm_g': near((L, 3, D_MODEL), 1.0),
        'w_ffn_up': nrm((L, 2, D_MODEL, 2 * D_FF), D_MODEL ** -0.5),
        'w_ffn_down': nrm((L, 2, D_FF, D_MODEL), D_FF ** -0.5),
        'w_in': nrm((L, D_MODEL, D_IN), D_MODEL ** -0.5),
        'conv_w': nrm((L, CONV_W, D_BR), CONV_W ** -0.5),
        'rwkv_mu': jax.random.uniform(keys.pop(), (L, D_RWKV_IN), f32),
        'rwkv_w0': nrm((L, D_BR), 1.0),
        'rwkv_w2': nrm((L, RANK_W, D_BR), 0.1 * RANK_W ** -0.5),
        'rwkv_a0': nrm((L, D_BR), 0.1),
        'rwkv_a2': nrm((L, RANK_A, D_BR), RANK_A ** -0.5),
        'rwkv_g2': nrm((L, RANK_G, D_BR), RANK_G ** -0.5),
        'rwkv_k_k': near((L, D_BR), 0.85),
        'rwkv_k_a': near((L, D_BR), 1.0),
        'rwkv_r_k': nrm((L, D_BR), 0.1),
        'rwkv_lnx_g': near((L, D_BR), 1.0),
        'rwkv_lnx_b': nrm((L, D_BR), 0.02),
        'pool_w': nrm((L, len(POOL_WINDOWS), POOL_GROUP, POOL_GROUP), POOL_GROUP ** -0.5),
        'pool_scale': near((L, D_BR), 1.0),
        'q_norm_g': near((L, HEAD_DIM), 1.0),
        'k_norm_g': near((L, HEAD_DIM), 1.0),
        'sb_bias': SB_BIAS_INIT + nrm((L, N_HEADS), 0.5),
        'w_gate': nrm((L, N_BRANCH, D_MODEL, D_MODEL), D_MODEL ** -0.5),
        'w_branch': nrm((L, N_BRANCH, D_BR, D_MODEL), D_BR ** -0.5),
        'w_out': nrm((L, D_MODEL, D_MODEL), D_MODEL ** -0.5),
    }


def reference(x_prompt, x_sample, c_prompt, c_sample, cache_k, cache_v, page_table,
              state_conv, state_shift, state_wkv, state_pool,
              ada_w, ada_b, norm_g, w_ffn_up, w_ffn_down, w_in, conv_w,
              rwkv_mu, rwkv_w0, rwkv_w2, rwkv_a0, rwkv_a2, rwkv_g2, rwkv_k_k, rwkv_k_a, rwkv_r_k,
              rwkv_lnx_g, rwkv_lnx_b, pool_w, pool_scale, q_norm_g, k_norm_g, sb_bias,
              w_gate, w_branch, w_out):
    P = {'ada_w': ada_w, 'ada_b': ada_b, 'norm_g': norm_g, 'w_ffn_up': w_ffn_up,
         'w_ffn_down': w_ffn_down, 'w_in': w_in, 'conv_w': conv_w, 'rwkv_mu': rwkv_mu,
         'rwkv_w0': rwkv_w0, 'rwkv_w2': rwkv_w2, 'rwkv_a0': rwkv_a0, 'rwkv_a2': rwkv_a2,
         'rwkv_g2': rwkv_g2, 'rwkv_k_k': rwkv_k_k, 'rwkv_k_a': rwkv_k_a, 'rwkv_r_k': rwkv_r_k,
         'rwkv_lnx_g': rwkv_lnx_g, 'rwkv_lnx_b': rwkv_lnx_b, 'pool_w': pool_w,
         'pool_scale': pool_scale, 'q_norm_g': q_norm_g, 'k_norm_g': k_norm_g, 'sb_bias': sb_bias,
         'w_gate': w_gate, 'w_branch': w_branch, 'w_out': w_out}
    bp = x_prompt.shape[0]
    bs = x_sample.shape[0]
    n_pages = page_table.shape[1]
    past_len = n_pages * PAGE_SIZE
    dt = x_prompt.dtype
    hp, hs = x_prompt, x_sample
    new_p = [[] for _ in range(6)]
    new_s = [[] for _ in range(6)]
    for l in range(DEPTH):
        hp, st_p = trunk_layer(
            l, hp, c_prompt, 0,
            jnp.zeros((bp, 0, N_HEADS, HEAD_DIM), dt), jnp.zeros((bp, 0, N_HEADS, HEAD_DIM), dt),
            jnp.zeros((bp, CONV_W - 1, D_BR), dt), jnp.zeros((bp, D_RWKV_IN), dt),
            jnp.zeros((bp, N_HEADS, HEAD_DIM, HEAD_DIM), jnp.float32),
            jnp.zeros((bp, POOL_CTX, D_BR), dt), P)
        k_past = cache_k[l][page_table].reshape(bs, past_len, N_HEADS, HEAD_DIM)
        v_past = cache_v[l][page_table].reshape(bs, past_len, N_HEADS, HEAD_DIM)
        hs, st_s = trunk_layer(l, hs, c_sample, past_len, k_past, v_past, state_conv[l],
                               state_shift[l], state_wkv[l], state_pool[l], P)
        for lst, a in zip(new_p, st_p):
            lst.append(a)
        for lst, a in zip(new_s, st_s):
            lst.append(a)
    k_prompt, v_prompt, conv_prompt, shift_prompt, wkv_prompt, pool_prompt = [jnp.stack(a, axis=0) for a in new_p]
    k_sample, v_sample, conv_sample, shift_sample, wkv_sample, pool_sample = [jnp.stack(a, axis=0) for a in new_s]
    return (hp, hs, k_prompt, v_prompt, conv_prompt, shift_prompt, wkv_prompt, pool_prompt,
            k_sample, v_sample, conv_sample, shift_sample, wkv_sample, pool_sample)
```

```python
import functools

import jax
import jax.numpy as jnp
from jax import lax
from jax.experimental import pallas as pl
from jax.experimental.pallas import tpu as pltpu

F32 = jnp.float32
BF16 = jnp.bfloat16

D_MODEL = 1024
N_BRANCH = 4
D_BR = 256
HEAD_DIM = 64
N_HEADS = 4
CONV_W = 3
POOL_WINDOWS = (2, 4, 8, 16)
POOL_CTX = 15
D_FF = 2816
D_IN = 2816
N_MOD = 9
PAGE = 128
EPS = 1e-6
LNX_EPS = 64e-5

HALO = 16
EXT_W = D_BR + D_MODEL + D_BR
RW_W = 7 * D_BR
RWKV_CHUNK = 64
VMEM_LIMIT = 56 * 1024 * 1024


def _cparams(sem):
    return pltpu.CompilerParams(dimension_semantics=sem, vmem_limit_bytes=VMEM_LIMIT)


def _resident(shape, index_map):
    return pl.BlockSpec(shape, index_map, pipeline_mode=pl.Buffered(1))


def _mm(a, b):
    return jnp.dot(a.astype(BF16), b.astype(BF16), preferred_element_type=F32)


def _mm_nt(a, b):
    return lax.dot_general(a.astype(BF16), b.astype(BF16), (((1,), (1,)), ((), ())),
                           preferred_element_type=F32)


def _mm_tn(a, b):
    return lax.dot_general(a.astype(BF16), b.astype(BF16), (((0,), (0,)), ((), ())),
                           preferred_element_type=F32)


def _split(a):
    hi = a.astype(BF16)
    lo = (a - hi.astype(F32)).astype(BF16)
    return hi, lo


def _mm3(a, b):
    ah, al = _split(a)
    bh, bl = _split(b)
    out = jnp.dot(ah, bh, preferred_element_type=F32)
    out = out + jnp.dot(ah, bl, preferred_element_type=F32)
    return out + jnp.dot(al, bh, preferred_element_type=F32)


def _softplus(y):
    return jnp.maximum(y, 0.0) + jnp.log1p(jnp.exp(-jnp.abs(y)))


def _sigmoid(y):
    return 1.0 / (1.0 + jnp.exp(-y))


def _head_sum_matrix():
    r = lax.broadcasted_iota(jnp.int32, (D_BR, D_BR), 0) >> 6
    c = lax.broadcasted_iota(jnp.int32, (D_BR, D_BR), 1) >> 6
    return jnp.where(r == c, 1.0, 0.0).astype(BF16)


def _modulated_norm(x, mod_ref, ng_ref, mi):
    sh = mod_ref[:, 3 * mi:3 * mi + 1, :]
    sc = mod_ref[:, 3 * mi + 1:3 * mi + 2, :]
    ms = jnp.mean(x * x, axis=-1, keepdims=True)
    u = x * lax.rsqrt(ms + EPS) * ng_ref[...]
    return u * (1.0 + sc) + sh


def _ada_kernel(c_ref, w_ref, b_ref, o_ref):
    c = c_ref[...]
    s = c * _sigmoid(c)
    o_ref[0] = _mm3(s, w_ref[0]) + b_ref[0]


def _ada_call(c_all, ada_w, ada_b):
    depth = ada_w.shape[0]
    rows = c_all.shape[0]
    n_out = ada_w.shape[2]
    tn = 1152
    return pl.pallas_call(
        _ada_kernel,
        grid=(depth, n_out // tn),
        in_specs=[pl.BlockSpec((rows, D_MODEL), lambda l, j: (0, 0)),
                  pl.BlockSpec((1, D_MODEL, tn), lambda l, j: (l, 0, j)),
                  pl.BlockSpec((1, 1, tn), lambda l, j: (l, 0, j))],
        out_specs=pl.BlockSpec((1, rows, tn), lambda l, j: (l, 0, j)),
        out_shape=jax.ShapeDtypeStruct((depth, rows, n_out), F32),
        compiler_params=_cparams(("parallel", "parallel")),
        name="ada_mod",
    )(c_all, ada_w, ada_b.reshape(depth, 1, n_out))


def _ffn_kernel(x_ref, mod_ref, ng_ref, wup_ref, wdn_ref, o_ref, *, mi, nf):
    x = x_ref[...]
    bb, tm, _ = x.shape
    u = _modulated_norm(x, mod_ref, ng_ref, mi)
    ub = u.reshape(bb * tm, D_MODEL).astype(BF16)
    tf = D_FF // nf
    acc = jnp.zeros((bb * tm, D_MODEL), F32)
    for j in range(nf):
        a = jnp.dot(ub, wup_ref[:, j * tf:(j + 1) * tf], preferred_element_type=F32)
        b = jnp.dot(ub, wup_ref[:, D_FF + j * tf:D_FF + (j + 1) * tf], preferred_element_type=F32)
        h = (a * _sigmoid(a) * b).astype(BF16)
        acc = acc + jnp.dot(h, wdn_ref[j * tf:(j + 1) * tf, :], preferred_element_type=F32)
    g = mod_ref[:, 3 * mi + 2:3 * mi + 3, :]
    o_ref[...] = x + 0.5 * g * acc.reshape(bb, tm, D_MODEL)


def _ffn_call(x, mod, ng, wup, wdn, l, which, bb, tm):
    bsz, t, _ = x.shape
    mi = 0 if which == 0 else 2
    kern = functools.partial(_ffn_kernel, mi=mi, nf=2)
    return pl.pallas_call(
        kern,
        grid=(bsz // bb, t // tm),
        in_specs=[pl.BlockSpec((bb, tm, D_MODEL), lambda i, j: (i, j, 0)),
                  pl.BlockSpec((bb, N_MOD, D_MODEL), lambda i, j: (i, 0, 0)),
                  pl.BlockSpec((None, 1, D_MODEL), lambda i, j: (3 * l + mi, 0, 0)),
                  _resident((None, None, D_MODEL, 2 * D_FF), lambda i, j: (l, which, 0, 0)),
                  _resident((None, None, D_FF, D_MODEL), lambda i, j: (l, which, 0, 0))],
        out_specs=pl.BlockSpec((bb, tm, D_MODEL), lambda i, j: (i, j, 0)),
        out_shape=jax.ShapeDtypeStruct(x.shape, F32),
        compiler_params=_cparams(("parallel", "parallel")),
        name="ffn",
    )(x, mod, ng, wup, wdn)


def _mix_kernel(x_ref, mod_ref, ng_ref, win_ref, halo_ref, convw_ref, mu_ref, w0_ref, ww2_ref,
                a0_ref, wa2_ref, wg2_ref, kk_ref, ka_ref, poolw_ref, pools_ref, qg_ref, kg_ref,
                u_ref, ya_ref, yc_ref, rw_ref, q_ref, k_ref, v_ref, conv_ref, shift_ref, pool_ref,
                ext, *, tm, t_real, pos0, nt):
    t = pl.program_id(1)

    @pl.when(t == 0)
    def _():
        ext[:, 0:HALO, :] = halo_ref[...]

    x = x_ref[...]
    bb = x.shape[0]
    rows = bb * tm
    u = _modulated_norm(x, mod_ref, ng_ref, 1)
    ub = u.reshape(rows, D_MODEL).astype(BF16)
    u_ref[...] = ub.reshape(bb, tm, D_MODEL)
    proj = jnp.dot(ub, win_ref[...], preferred_element_type=F32)

    bg = proj[:, 0:D_BR]
    z = proj[:, D_BR:2 * D_BR] * proj[:, 2 * D_BR:3 * D_BR]
    p_rwkv = proj[:, 3 * D_BR:3 * D_BR + D_MODEL]
    p_pool = proj[:, 3 * D_BR + D_MODEL:4 * D_BR + D_MODEL]
    ext[:, HALO:HALO + tm, 0:D_BR] = z.reshape(bb, tm, D_BR)
    ext[:, HALO:HALO + tm, D_BR:D_BR + D_MODEL] = p_rwkv.reshape(bb, tm, D_MODEL)
    ext[:, HALO:HALO + tm, D_BR + D_MODEL:EXT_W] = p_pool.reshape(bb, tm, D_BR)
    cw = convw_ref[...]
    z1 = ext[:, HALO - 1:HALO - 1 + tm, 0:D_BR].reshape(rows, D_BR)
    z2 = ext[:, HALO - 2:HALO - 2 + tm, 0:D_BR].reshape(rows, D_BR)
    ya = bg * (cw[0:1] * z2 + cw[1:2] * z1 + cw[2:3] * z)
    ya_ref[...] = ya.astype(BF16).reshape(bb, tm, D_BR)

    pp = ext[:, HALO - 1:HALO - 1 + tm, D_BR:D_BR + D_MODEL].reshape(rows, D_MODEL)
    xs = p_rwkv + (pp - p_rwkv) * mu_ref[...]
    r = xs[:, 0:D_BR]
    k = xs[:, D_BR:2 * D_BR]
    v = xs[:, 2 * D_BR:3 * D_BR]
    wd = xs[:, 3 * D_BR:3 * D_BR + 64]
    ad = xs[:, 3 * D_BR + 64:3 * D_BR + 128]
    gd = xs[:, 3 * D_BR + 128:D_MODEL]
    w_log = -_softplus(-(w0_ref[...] + _mm(jnp.tanh(wd), ww2_ref[...]))) - 0.5
    logw = -jnp.exp(w_log)
    a = _sigmoid(a0_ref[...] + _mm(ad, wa2_ref[...]))
    g = _mm(_sigmoid(gd), wg2_ref[...])
    hsum = _head_sum_matrix()
    kkr = k * kk_ref[...]
    kk = kkr / jnp.maximum(jnp.sqrt(_mm(kkr * kkr, hsum)), 1e-12)
    kmod = k * (1.0 + (a - 1.0) * ka_ref[...])
    bvec = kk * a
    tok = t * tm + (lax.broadcasted_iota(jnp.int32, (rows, 1), 0) & (tm - 1))
    if t_real < nt * tm:
        live = jnp.where(tok < t_real, 1.0, 0.0)
        logw, kmod, v, kk, bvec = logw * live, kmod * live, v * live, kk * live, bvec * live
    for i, val in enumerate((r, logw, kmod, v, kk, bvec, g)):
        rw_ref[:, :, i * D_BR:(i + 1) * D_BR] = val.reshape(bb, tm, D_BR)

    col = lax.broadcasted_iota(jnp.int32, (1, D_BR), 1)
    acc = p_pool
    sel = None
    for i in range(1, POOL_WINDOWS[-1]):
        acc = acc + ext[:, HALO - i:HALO - i + tm, D_BR + D_MODEL:EXT_W].reshape(rows, D_BR)
        if (i + 1) in POOL_WINDOWS:
            gi = POOL_WINDOWS.index(i + 1)
            sel = acc if sel is None else jnp.where(col >= gi * 64, acc, sel)
    wcol = jnp.where(col < 64, 2, jnp.where(col < 128, 4, jnp.where(col < 192, 8, 16)))
    cnt = jnp.minimum(wcol, pos0 + tok + 1).astype(F32)
    dd = sel / cnt - p_pool
    yc = _mm(dd, poolw_ref[...]) * pools_ref[...]
    yc_ref[...] = yc.astype(BF16).reshape(bb, tm, D_BR)

    q = proj[:, 4 * D_BR + D_MODEL:5 * D_BR + D_MODEL]
    ks = proj[:, 5 * D_BR + D_MODEL:6 * D_BR + D_MODEL]
    vs = proj[:, 6 * D_BR + D_MODEL:D_IN]
    qn = q * lax.rsqrt(_mm(q * q, hsum) * (1.0 / HEAD_DIM) + EPS) * qg_ref[...]
    kn = ks * lax.rsqrt(_mm(ks * ks, hsum) * (1.0 / HEAD_DIM) + EPS) * kg_ref[...]
    q_ref[...] = qn.astype(BF16).reshape(bb, tm, D_BR)
    k_ref[...] = kn.reshape(bb, tm, D_BR)
    v_ref[...] = vs.reshape(bb, tm, D_BR)

    @pl.when(t == nt - 1)
    def _():
        last = HALO + t_real - (nt - 1) * tm - 1
        conv_ref[...] = ext[:, last - (CONV_W - 2):last + 1, 0:D_BR]
        shift_ref[...] = ext[:, last:last + 1, D_BR:D_BR + D_MODEL]
        pool_ref[...] = ext[:, last - (POOL_CTX - 1):last + 1, D_BR + D_MODEL:EXT_W]

    if nt > 1:
        ext[:, 0:HALO, :] = ext[:, tm:tm + HALO, :]


def _mix_call(x, mod, ng, win, halo0, p, l, bb, tm, t_real, pos0):
    bsz, t, _ = x.shape
    nt = t // tm
    kern = functools.partial(_mix_kernel, tm=tm, t_real=t_real, pos0=pos0, nt=nt)
    tok = lambda w: pl.BlockSpec((bb, tm, w), lambda i, j: (i, j, 0))
    per_b = lambda r, w: pl.BlockSpec((bb, r, w), lambda i, j: (i, 0, 0))
    row = lambda r, w: pl.BlockSpec((None, r, w), lambda i, j: (l, 0, 0))
    sds = lambda shape, dt: jax.ShapeDtypeStruct(shape, dt)
    return pl.pallas_call(
        kern,
        grid=(bsz // bb, nt),
        in_specs=[tok(D_MODEL), per_b(N_MOD, D_MODEL),
                  pl.BlockSpec((None, 1, D_MODEL), lambda i, j: (3 * l + 1, 0, 0)),
                  _resident((None, D_MODEL, D_IN), lambda i, j: (l, 0, 0)),
                  per_b(HALO, EXT_W),
                  row(CONV_W, D_BR), row(1, D_MODEL), row(1, D_BR), row(64, D_BR), row(1, D_BR),
                  row(64, D_BR), row(128, D_BR), row(1, D_BR), row(1, D_BR), row(D_BR, D_BR),
                  row(1, D_BR), row(1, D_BR), row(1, D_BR)],
        out_specs=[tok(D_MODEL), tok(D_BR), tok(D_BR), tok(RW_W), tok(D_BR), tok(D_BR), tok(D_BR),
                   per_b(CONV_W - 1, D_BR), per_b(1, D_MODEL), per_b(POOL_CTX, D_BR)],
        out_shape=[sds((bsz, t, D_MODEL), BF16), sds((bsz, t, D_BR), BF16), sds((bsz, t, D_BR), BF16),
                   sds((bsz, t, RW_W), F32), sds((bsz, t, D_BR), BF16), sds((bsz, t, D_BR), F32),
                   sds((bsz, t, D_BR), F32), sds((bsz, CONV_W - 1, D_BR), F32),
                   sds((bsz, 1, D_MODEL), F32), sds((bsz, POOL_CTX, D_BR), F32)],
        scratch_shapes=[pltpu.VMEM((bb, HALO + tm, EXT_W), F32)],
        compiler_params=_cparams(("parallel", "arbitrary")),
        name="mix_prep",
    )(x, mod, ng, win, halo0, p["conv_w"], p["mu"], p["w0"], p["w2"], p["a0"], p["a2"], p["g2"],
      p["k_k"], p["k_a"], p["pool_bd"], p["pool_scale"], p["q_g"], p["k_g"])


def _rwkv_kernel(rw_ref, s0_ref, rk_ref, lng_ref, lnb_ref, y_ref, sout_ref, s_scr, *, nc):
    c_idx = pl.program_id(1)

    @pl.when(c_idx == 0)
    def _():
        s_scr[...] = s0_ref[0]

    rw = rw_ref[0]
    c_in = rw.shape[0]
    if c_in < RWKV_CHUNK:
        rw = jnp.concatenate([rw, jnp.zeros((RWKV_CHUNK - c_in, RW_W), F32)], axis=0)
    csz = rw.shape[0]
    ri = lax.broadcasted_iota(jnp.int32, (csz, csz), 0)
    ci = lax.broadcasted_iota(jnp.int32, (csz, csz), 1)
    lower = ri >= ci
    strict = ri > ci
    logw_all = rw[:, D_BR:2 * D_BR]
    lh, ll = _split(logw_all)
    tril = jnp.where(lower, 1.0, 0.0).astype(BF16)
    cum_all = (jnp.dot(tril, lh, preferred_element_type=F32)
               + jnp.dot(tril, ll, preferred_element_type=F32))
    eye = jnp.where(ri == ci, 1.0, 0.0)
    levels = [((ri >> (s + 1)) == (ci >> (s + 1))) & ((ri >> s) != (ci >> s))
              for s in range(csz.bit_length() - 1)]

    for h in range(N_HEADS):
        sl = slice(h * HEAD_DIM, (h + 1) * HEAD_DIM)
        r = rw[:, 0 * D_BR:1 * D_BR][:, sl]
        logw = logw_all[:, sl]
        k = rw[:, 2 * D_BR:3 * D_BR][:, sl]
        v = rw[:, 3 * D_BR:4 * D_BR][:, sl]
        kk = rw[:, 4 * D_BR:5 * D_BR][:, sl]
        b = rw[:, 5 * D_BR:6 * D_BR][:, sl]
        g = rw[:, 6 * D_BR:7 * D_BR][:, sl]
        cum = cum_all[:, sl]
        tot = cum[csz - 1:csz, :]
        e_inv = jnp.exp(-cum)
        kt = kk * jnp.exp(cum - logw)
        bt = b * e_inv
        kkd = k * e_inv
        rt = r * jnp.exp(cum)
        e_rest = jnp.exp(tot - cum)
        bh = b * e_rest
        kh = k * e_rest

        a_b = jnp.where(strict, _mm_nt(kt, bt), 0.0)
        a_k = jnp.where(strict, _mm_nt(kt, kkd), 0.0)
        m_b = jnp.where(lower, _mm_nt(rt, bt), 0.0)
        m_k = jnp.where(lower, _mm_nt(rt, kkd), 0.0)
        tinv = eye - jnp.where(levels[0], a_b, 0.0)
        for lv in levels[1:]:
            tinv = tinv - _mm(tinv, _mm(jnp.where(lv, a_b, 0.0), tinv))
        w1 = _mm(tinv, kt)
        w2 = _mm(tinv, _mm(a_k, v))

        s_prev = s_scr[h]
        u = _mm_nt(w1, s_prev) + w2
        y = _mm_nt(rt, s_prev) + _mm(m_k, v) - _mm(m_b, u)
        s_scr[h] = s_prev * jnp.exp(tot) + _mm_tn(v, kh) - _mm_tn(u, bh)

        mean = jnp.mean(y, axis=-1, keepdims=True)
        yc = y - mean
        var = jnp.mean(yc * yc, axis=-1, keepdims=True)
        yn = yc * lax.rsqrt(var + LNX_EPS) * lng_ref[:, sl] + lnb_ref[:, sl]
        bonus = jnp.sum(r * k * rk_ref[:, sl], axis=-1, keepdims=True) * v
        y_ref[0, :, sl] = ((yn + bonus) * g)[0:c_in].astype(BF16)

    @pl.when(c_idx == nc - 1)
    def _():
        sout_ref[0] = s_scr[...]


def _rwkv_call(rw, s0, p, l, csz):
    bsz, t, _ = rw.shape
    nc = t // csz
    row = lambda: pl.BlockSpec((None, 1, D_BR), lambda i, j: (l, 0, 0))
    st = pl.BlockSpec((1, N_HEADS, HEAD_DIM, HEAD_DIM), lambda i, j: (i, 0, 0, 0))
    return pl.pallas_call(
        functools.partial(_rwkv_kernel, nc=nc),
        grid=(bsz, nc),
        in_specs=[pl.BlockSpec((1, csz, RW_W), lambda i, j: (i, j, 0)), st, row(), row(), row()],
        out_specs=[pl.BlockSpec((1, csz, D_BR), lambda i, j: (i, j, 0)), st],
        out_shape=[jax.ShapeDtypeStruct((bsz, t, D_BR), BF16),
                   jax.ShapeDtypeStruct((bsz, N_HEADS, HEAD_DIM, HEAD_DIM), F32)],
        scratch_shapes=[pltpu.VMEM((N_HEADS, HEAD_DIM, HEAD_DIM), F32)],
        compiler_params=_cparams(("parallel", "arbitrary")),
        name="rwkv",
    )(rw, s0, p["r_k"], p["lnx_g"], p["lnx_b"])


def _sb_block(z, causal, tri, carry):
    sp = _softplus(z)
    lk = -sp
    if causal is not None:
        lk = jnp.where(causal, lk, 0.0)
    after = jnp.dot(lk.astype(BF16), tri, preferred_element_type=F32)
    a = jnp.exp(z - sp + after + carry)
    if causal is not None:
        a = jnp.where(causal, a, 0.0)
    return a, carry + after[:, 0:1] + lk[:, 0:1]


def _sba_prompt_kernel(bias_ref, q_ref, k_ref, v_ref, tri_ref, o_ref, acc, carry, *, nk, tq):
    qi = pl.program_id(1)
    j = pl.program_id(2)

    @pl.when(j == 0)
    def _():
        acc[...] = jnp.zeros_like(acc)
        carry[...] = jnp.zeros_like(carry)

    def compute(diag):
        q = q_ref[0]
        kb = k_ref[0].astype(BF16)
        vb = v_ref[0].astype(BF16)
        tri = tri_ref[...]
        causal = None
        if diag:
            causal = (lax.broadcasted_iota(jnp.int32, (tq, tq), 1)
                      < lax.broadcasted_iota(jnp.int32, (tq, tq), 0))
        for h in range(N_HEADS):
            sl = slice(h * HEAD_DIM, (h + 1) * HEAD_DIM)
            z = lax.dot_general(q[:, sl], kb[:, sl], (((1,), (1,)), ((), ())),
                                preferred_element_type=F32) * (HEAD_DIM ** -0.5) + bias_ref[h]
            a, c_new = _sb_block(z, causal, tri, carry[h])
            carry[h] = c_new
            acc[:, sl] += jnp.dot(a.astype(BF16), vb[:, sl], preferred_element_type=F32)

    @pl.when(j == 0)
    def _():
        compute(True)

    @pl.when((j > 0) & (j <= qi))
    def _():
        compute(False)

    @pl.when(j == nk - 1)
    def _():
        o_ref[0] = acc[...].astype(BF16)


def _sba_prompt_call(q, k, v, bias, tri, tq):
    bsz, t, _ = q.shape
    nq = t // tq
    kv_map = lambda b, i, j: (b, jnp.maximum(i - j, 0), 0)
    return pl.pallas_call(
        functools.partial(_sba_prompt_kernel, nk=nq, tq=tq),
        grid=(bsz, nq, nq),
        in_specs=[pl.BlockSpec(memory_space=pltpu.SMEM),
                  pl.BlockSpec((1, tq, D_BR), lambda b, i, j: (b, i, 0)),
                  pl.BlockSpec((1, tq, D_BR), kv_map),
                  pl.BlockSpec((1, tq, D_BR), kv_map),
                  _resident((tq, tq), lambda b, i, j: (0, 0))],
        out_specs=pl.BlockSpec((1, tq, D_BR), lambda b, i, j: (b, i, 0)),
        out_shape=jax.ShapeDtypeStruct((bsz, t, D_BR), BF16),
        scratch_shapes=[pltpu.VMEM((tq, D_BR), F32), pltpu.VMEM((N_HEADS, tq, 1), F32)],
        compiler_params=_cparams(("parallel", "parallel", "arbitrary")),
        name="sba_prompt",
    )(bias, q, k, v, tri)


def _sba_sample_kernel(pt_ref, bias_ref, q_ref, kn_ref, vn_ref, tri_ref, *rest, npg, tpad, nsteps):
    k_refs = rest[:npg]
    v_refs = rest[npg:2 * npg]
    o_ref = rest[2 * npg]
    acc, carry = rest[2 * npg + 1:]
    s = pl.program_id(1)
    rows = N_HEADS * tpad
    tri = tri_ref[...]
    tshift = tpad.bit_length() - 1
    rid = lax.broadcasted_iota(jnp.int32, (rows, D_BR), 0) >> tshift
    cid = lax.broadcasted_iota(jnp.int32, (rows, D_BR), 1) >> 6
    head_match = rid == cid
    q = q_ref[0].astype(F32)
    q_bd = jnp.where(head_match, jnp.concatenate([q] * N_HEADS, axis=0), 0.0).astype(BF16)
    hrow = lax.broadcasted_iota(jnp.int32, (rows, 1), 0) >> tshift
    bias = jnp.zeros((rows, 1), F32)
    for h in range(N_HEADS):
        bias = jnp.where(hrow == h, bias_ref[h], bias)

    def block(kb, vb, causal):
        z = lax.dot_general(q_bd, kb, (((1,), (1,)), ((), ())),
                            preferred_element_type=F32) * (HEAD_DIM ** -0.5) + bias
        a, c_new = _sb_block(z, causal, tri, carry[...])
        carry[...] = c_new
        acc[...] += jnp.dot(a.astype(BF16), vb, preferred_element_type=F32)

    @pl.when(s == 0)
    def _():
        acc[...] = jnp.zeros_like(acc)
        carry[...] = jnp.zeros_like(carry)
        kpos = lax.broadcasted_iota(jnp.int32, (rows, PAGE), 1)
        qpos = lax.broadcasted_iota(jnp.int32, (rows, PAGE), 0) & (tpad - 1)
        block(kn_ref[0].astype(BF16), vn_ref[0].astype(BF16), kpos < qpos)

    for i in range(npg - 1, -1, -1):
        block(k_refs[i][0, 0].astype(BF16), v_refs[i][0, 0].astype(BF16), None)

    @pl.when(s == nsteps - 1)
    def _():
        masked = jnp.where(head_match, acc[...], 0.0)
        out = masked[0:tpad]
        for h in range(1, N_HEADS):
            out = out + masked[h * tpad:(h + 1) * tpad]
        o_ref[0] = out.astype(BF16)


def _sba_sample_call(q, kn, vn, cache_k, cache_v, page_table, bias, tri, l, npg):
    bsz, tpad, _ = q.shape
    n_pages = page_table.shape[1]
    nsteps = n_pages // npg
    rows = N_HEADS * tpad

    def page_spec(i):
        return pl.BlockSpec((1, 1, PAGE, D_BR),
                            lambda b, s, pt: (l, pt[b, n_pages - (s + 1) * npg + i], 0, 0))

    grid_spec = pltpu.PrefetchScalarGridSpec(
        num_scalar_prefetch=1,
        grid=(bsz, nsteps),
        in_specs=[pl.BlockSpec(memory_space=pltpu.SMEM),
                  pl.BlockSpec((1, tpad, D_BR), lambda b, s, pt: (b, 0, 0)),
                  pl.BlockSpec((1, PAGE, D_BR), lambda b, s, pt: (b, 0, 0)),
                  pl.BlockSpec((1, PAGE, D_BR), lambda b, s, pt: (b, 0, 0)),
                  pl.BlockSpec((PAGE, PAGE), lambda b, s, pt: (0, 0))]
        + [page_spec(i) for i in range(npg)] + [page_spec(i) for i in range(npg)],
        out_specs=pl.BlockSpec((1, tpad, D_BR), lambda b, s, pt: (b, 0, 0)),
        scratch_shapes=[pltpu.VMEM((rows, D_BR), F32), pltpu.VMEM((rows, 1), F32)],
    )
    return pl.pallas_call(
        functools.partial(_sba_sample_kernel, npg=npg, tpad=tpad, nsteps=nsteps),
        grid_spec=grid_spec,
        out_shape=jax.ShapeDtypeStruct((bsz, tpad, D_BR), BF16),
        compiler_params=_cparams(("parallel", "arbitrary")),
        name="sba_sample",
    )(page_table, bias, q, kn, vn, tri, *([cache_k] * npg), *([cache_v] * npg))


def _merge_kernel(x_ref, mod_ref, u_ref, ya_ref, yb_ref, yc_ref, yd_ref, wg_ref, wb_ref, wo_ref, o_ref):
    x = x_ref[...]
    bb, tm, _ = x.shape
    rows = bb * tm
    ub = u_ref[...].reshape(rows, D_MODEL)
    merged = jnp.zeros((rows, D_MODEL), F32)
    for i, y_ref in enumerate((ya_ref, yb_ref, yc_ref, yd_ref)):
        gate = _sigmoid(jnp.dot(ub, wg_ref[i], preferred_element_type=F32))
        br = jnp.dot(y_ref[...].reshape(rows, D_BR), wb_ref[i], preferred_element_type=F32)
        merged = merged + gate * br
    out = jnp.dot(merged.astype(BF16), wo_ref[...], preferred_element_type=F32)
    g2 = mod_ref[:, 5:6, :]
    o_ref[...] = x + g2 * out.reshape(bb, tm, D_MODEL)


def _merge_call(x, mod, ub, ya, yb, yc, yd, wg, wb, wo, l, bb, tm):
    bsz, t, _ = x.shape
    tok = lambda w: pl.BlockSpec((bb, tm, w), lambda i, j: (i, j, 0))
    return pl.pallas_call(
        _merge_kernel,
        grid=(bsz // bb, t // tm),
        in_specs=[tok(D_MODEL), pl.BlockSpec((bb, N_MOD, D_MODEL), lambda i, j: (i, 0, 0)),
                  tok(D_MODEL), tok(D_BR), tok(D_BR), tok(D_BR), tok(D_BR),
                  _resident((None, N_BRANCH, D_MODEL, D_MODEL), lambda i, j: (l, 0, 0, 0)),
                  _resident((None, N_BRANCH, D_BR, D_MODEL), lambda i, j: (l, 0, 0, 0)),
                  _resident((None, D_MODEL, D_MODEL), lambda i, j: (l, 0, 0))],
        out_specs=tok(D_MODEL),
        out_shape=jax.ShapeDtypeStruct(x.shape, F32),
        compiler_params=_cparams(("parallel", "parallel")),
        name="merge",
    )(x, mod, ub, ya, yb, yc, yd, wg, wb, wo)


def _tri_matrix(n):
    r = lax.broadcasted_iota(jnp.int32, (n, n), 0)
    c = lax.broadcasted_iota(jnp.int32, (n, n), 1)
    return jnp.where(r > c, 1.0, 0.0).astype(BF16)


def kernel(x_prompt, x_sample, c_prompt, c_sample, cache_k, cache_v, page_table, state_conv, state_shift, state_wkv, state_pool, ada_w, ada_b, norm_g, w_ffn_up, w_ffn_down, w_in, conv_w, rwkv_mu, rwkv_w0, rwkv_w2, rwkv_a0, rwkv_a2, rwkv_g2, rwkv_k_k, rwkv_k_a, rwkv_r_k, rwkv_lnx_g, rwkv_lnx_b, pool_w, pool_scale, q_norm_g, k_norm_g, sb_bias, w_gate, w_branch, w_out):
    depth = ada_w.shape[0]
    bp, seq, _ = x_prompt.shape
    bs, dec_seq, _ = x_sample.shape
    n_phys = cache_k.shape[1]
    past_len = page_table.shape[1] * PAGE
    tpad = 8
    tm_p = 512
    tq_p = 512

    wup = w_ffn_up.astype(BF16)
    wdn = w_ffn_down.astype(BF16)
    win = w_in.astype(BF16)
    wg = w_gate.astype(BF16)
    wb = w_branch.astype(BF16)
    wo = w_out.astype(BF16)
    ng = norm_g.reshape(depth * 3, 1, D_MODEL)
    pool_bd = jnp.zeros((depth, len(POOL_WINDOWS), 64, len(POOL_WINDOWS), 64), F32)
    for gi in range(len(POOL_WINDOWS)):
        pool_bd = pool_bd.at[:, gi, :, gi, :].set(pool_w[:, gi])
    prm = {
        "conv_w": conv_w, "mu": rwkv_mu[:, None, :], "w0": rwkv_w0[:, None, :], "w2": rwkv_w2,
        "a0": rwkv_a0[:, None, :], "a2": rwkv_a2, "g2": rwkv_g2, "k_k": rwkv_k_k[:, None, :],
        "k_a": rwkv_k_a[:, None, :], "pool_bd": pool_bd.reshape(depth, D_BR, D_BR).astype(BF16),
        "pool_scale": pool_scale[:, None, :],
        "q_g": jnp.tile(q_norm_g, (1, N_HEADS))[:, None, :],
        "k_g": jnp.tile(k_norm_g, (1, N_HEADS))[:, None, :],
        "r_k": rwkv_r_k[:, None, :], "lnx_g": rwkv_lnx_g[:, None, :], "lnx_b": rwkv_lnx_b[:, None, :],
    }
    ck = cache_k.reshape(depth, n_phys, PAGE, D_BR)
    cv = cache_v.reshape(depth, n_phys, PAGE, D_BR)
    tri_p = _tri_matrix(tq_p)
    tri_s = _tri_matrix(PAGE)

    n_c = bp + bs
    c_all = jnp.concatenate([c_prompt, c_sample, jnp.zeros((-n_c % 8, D_MODEL), F32)], axis=0)
    mod_all = _ada_call(c_all, ada_w, ada_b)
    xs_pad = jnp.concatenate([x_sample, jnp.zeros((bs, tpad - dec_seq, D_MODEL), F32)], axis=1)

    halo_p = jnp.zeros((bp, HALO, EXT_W), F32)
    wkv0_p = jnp.zeros((bp, N_HEADS, HEAD_DIM, HEAD_DIM), F32)

    hp, hs = x_prompt, xs_pad
    outs_p = [[] for _ in range(6)]
    outs_s = [[] for _ in range(6)]
    for l in range(depth):
        mod_p = mod_all[l, :bp].reshape(bp, N_MOD, D_MODEL)
        mod_s = mod_all[l, bp:n_c].reshape(bs, N_MOD, D_MODEL)
        halo_s = jnp.zeros((bs, HALO, EXT_W), F32)
        halo_s = halo_s.at[:, HALO - (CONV_W - 1):, 0:D_BR].set(state_conv[l])
        halo_s = halo_s.at[:, HALO - 1, D_BR:D_BR + D_MODEL].set(state_shift[l])
        halo_s = halo_s.at[:, HALO - POOL_CTX:, D_BR + D_MODEL:].set(state_pool[l])

        for grp in ("p", "s"):
            if grp == "p":
                x, mod, bb, tm, t_real, pos0, halo0, wkv0 = hp, mod_p, 1, tm_p, seq, 0, halo_p, wkv0_p
            else:
                x, mod, bb, tm, t_real, pos0, halo0, wkv0 = hs, mod_s, bs, tpad, dec_seq, past_len, halo_s, state_wkv[l]
            x = _ffn_call(x, mod, ng, wup, wdn, l, 0, bb, tm)
            ub, ya, yc, rw, q, k, v, conv_n, shift_n, pool_n = _mix_call(
                x, mod, ng, win, halo0, prm, l, bb, tm, t_real, pos0)
            yb, wkv_n = _rwkv_call(rw, wkv0, prm, l, RWKV_CHUNK if grp == "p" else tpad)
            if grp == "p":
                yd = _sba_prompt_call(q, k, v, sb_bias[l], tri_p, tq_p)
            else:
                kn = jnp.concatenate([k, jnp.zeros((bs, PAGE - tpad, D_BR), F32)], axis=1)
                vn = jnp.concatenate([v, jnp.zeros((bs, PAGE - tpad, D_BR), F32)], axis=1)
                yd = _sba_sample_call(q, kn, vn, ck, cv, page_table, sb_bias[l], tri_s, l, 8)
            x = _merge_call(x, mod, ub, ya, yb, yc, yd, wg, wb, wo, l, bb, tm)
            x = _ffn_call(x, mod, ng, wup, wdn, l, 1, bb, tm)
            new = (k[:, :t_real].reshape(-1, t_real, N_HEADS, HEAD_DIM),
                   v[:, :t_real].reshape(-1, t_real, N_HEADS, HEAD_DIM),
                   conv_n, shift_n[:, 0], wkv_n, pool_n)
            if grp == "p":
                hp = x
                for lst, a in zip(outs_p, new):
                    lst.append(a)
            else:
                hs = x
                for lst, a in zip(outs_s, new):
                    lst.append(a)

    stacked_p = [jnp.stack(a, axis=0) for a in outs_p]
    stacked_s = [jnp.stack(a, axis=0) for a in outs_s]
    return (hp, hs[:, :dec_seq], *stacked_p, *stacked_s)
```

```python
import functools

import jax
import jax.numpy as jnp
from jax import lax
from jax.experimental import pallas as pl
from jax.experimental.pallas import tpu as pltpu

F32 = jnp.float32
BF16 = jnp.bfloat16

D_MODEL = 1024
N_BRANCH = 4
D_BR = 256
HEAD_DIM = 64
N_HEADS = 4
CONV_W = 3
POOL_WINDOWS = (2, 4, 8, 16)
POOL_CTX = 15
D_FF = 2816
D_IN = 2816
N_MOD = 9
PAGE = 128
EPS = 1e-6
LNX_EPS = 64e-5

HALO = 16
EXT_W = D_BR + D_MODEL + D_BR
RW_W = 7 * D_BR
RWKV_CHUNK = 64
VMEM_LIMIT = 56 * 1024 * 1024


def _cparams(sem):
    return pltpu.CompilerParams(dimension_semantics=sem, vmem_limit_bytes=VMEM_LIMIT)


def _resident(shape, index_map):
    return pl.BlockSpec(shape, index_map, pipeline_mode=pl.Buffered(1))


def _mm(a, b):
    return jnp.dot(a.astype(BF16), b.astype(BF16), preferred_element_type=F32)


def _mm_nt(a, b):
    return lax.dot_general(a.astype(BF16), b.astype(BF16), (((1,), (1,)), ((), ())),
                           preferred_element_type=F32)


def _mm_tn(a, b):
    return lax.dot_general(a.astype(BF16), b.astype(BF16), (((0,), (0,)), ((), ())),
                           preferred_element_type=F32)


def _split(a):
    hi = a.astype(BF16)
    lo = (a - hi.astype(F32)).astype(BF16)
    return hi, lo


def _mm3(a, b):
    ah, al = _split(a)
    bh, bl = _split(b)
    out = jnp.dot(ah, bh, preferred_element_type=F32)
    out = out + jnp.dot(ah, bl, preferred_element_type=F32)
    return out + jnp.dot(al, bh, preferred_element_type=F32)


def _softplus(y):
    return jnp.maximum(y, 0.0) + jnp.log(1.0 + jnp.exp(-jnp.abs(y)))


def _sigmoid(y):
    return 1.0 / (1.0 + jnp.exp(-y))


def _head_sum_matrix():
    r = lax.broadcasted_iota(jnp.int32, (D_BR, D_BR), 0) >> 6
    c = lax.broadcasted_iota(jnp.int32, (D_BR, D_BR), 1) >> 6
    return jnp.where(r == c, 1.0, 0.0).astype(BF16)


def _modulated_norm(x, mod_ref, ng_ref, mi):
    sh = mod_ref[:, 3 * mi:3 * mi + 1, :]
    sc = mod_ref[:, 3 * mi + 1:3 * mi + 2, :]
    ms = jnp.mean(x * x, axis=-1, keepdims=True)
    u = x * lax.rsqrt(ms + EPS) * ng_ref[...]
    return u * (1.0 + sc) + sh


def _ada_kernel(c_ref, w_ref, b_ref, o_ref):
    c = c_ref[...]
    s = c * _sigmoid(c)
    o_ref[0] = _mm3(s, w_ref[0]) + b_ref[0]


def _ada_call(c_all, ada_w, ada_b):
    depth = ada_w.shape[0]
    rows = c_all.shape[0]
    n_out = ada_w.shape[2]
    tn = 1152
    return pl.pallas_call(
        _ada_kernel,
        grid=(depth, n_out // tn),
        in_specs=[pl.BlockSpec((rows, D_MODEL), lambda l, j: (0, 0)),
                  pl.BlockSpec((1, D_MODEL, tn), lambda l, j: (l, 0, j)),
                  pl.BlockSpec((1, 1, tn), lambda l, j: (l, 0, j))],
        out_specs=pl.BlockSpec((1, rows, tn), lambda l, j: (l, 0, j)),
        out_shape=jax.ShapeDtypeStruct((depth, rows, n_out), F32),
        compiler_params=_cparams(("parallel", "parallel")),
        name="ada_mod",
    )(c_all, ada_w, ada_b.reshape(depth, 1, n_out))


def _ffn_kernel(x_ref, mod_ref, ng_ref, wup_ref, wdn_ref, o_ref, *, mi, nf):
    x = x_ref[...]
    bb, tm, _ = x.shape
    u = _modulated_norm(x, mod_ref, ng_ref, mi)
    ub = u.reshape(bb * tm, D_MODEL).astype(BF16)
    tf = D_FF // nf
    acc = jnp.zeros((bb * tm, D_MODEL), F32)
    for j in range(nf):
        a = jnp.dot(ub, wup_ref[:, j * tf:(j + 1) * tf], preferred_element_type=F32)
        b = jnp.dot(ub, wup_ref[:, D_FF + j * tf:D_FF + (j + 1) * tf], preferred_element_type=F32)
        h = (a * _sigmoid(a) * b).astype(BF16)
        acc = acc + jnp.dot(h, wdn_ref[j * tf:(j + 1) * tf, :], preferred_element_type=F32)
    g = mod_ref[:, 3 * mi + 2:3 * mi + 3, :]
    o_ref[...] = x + 0.5 * g * acc.reshape(bb, tm, D_MODEL)


def _ffn_call(x, mod, ng, wup, wdn, l, which, bb, tm):
    bsz, t, _ = x.shape
    mi = 0 if which == 0 else 2
    kern = functools.partial(_ffn_kernel, mi=mi, nf=2)
    return pl.pallas_call(
        kern,
        grid=(bsz // bb, t // tm),
        in_specs=[pl.BlockSpec((bb, tm, D_MODEL), lambda i, j: (i, j, 0)),
                  pl.BlockSpec((bb, N_MOD, D_MODEL), lambda i, j: (i, 0, 0)),
                  pl.BlockSpec((None, 1, D_MODEL), lambda i, j: (3 * l + mi, 0, 0)),
                  _resident((None, None, D_MODEL, 2 * D_FF), lambda i, j: (l, which, 0, 0)),
                  _resident((None, None, D_FF, D_MODEL), lambda i, j: (l, which, 0, 0))],
        out_specs=pl.BlockSpec((bb, tm, D_MODEL), lambda i, j: (i, j, 0)),
        out_shape=jax.ShapeDtypeStruct(x.shape, F32),
        compiler_params=_cparams(("parallel", "parallel")),
        name="ffn",
    )(x, mod, ng, wup, wdn)


def _mix_kernel(x_ref, mod_ref, ng_ref, win_ref, halo_ref, convw_ref, mu_ref, w0_ref, ww2_ref,
                a0_ref, wa2_ref, wg2_ref, kk_ref, ka_ref, poolw_ref, pools_ref, qg_ref, kg_ref,
                u_ref, ya_ref, yc_ref, rw_ref, q_ref, k_ref, v_ref, conv_ref, shift_ref, pool_ref,
                ext, *, tm, t_real, pos0, nt):
    t = pl.program_id(1)

    @pl.when(t == 0)
    def _():
        ext[:, 0:HALO, :] = halo_ref[...]

    x = x_ref[...]
    bb = x.shape[0]
    rows = bb * tm
    u = _modulated_norm(x, mod_ref, ng_ref, 1)
    ub = u.reshape(rows, D_MODEL).astype(BF16)
    u_ref[...] = ub.reshape(bb, tm, D_MODEL)
    proj = jnp.dot(ub, win_ref[...], preferred_element_type=F32)

    bg = proj[:, 0:D_BR]
    z = proj[:, D_BR:2 * D_BR] * proj[:, 2 * D_BR:3 * D_BR]
    p_rwkv = proj[:, 3 * D_BR:3 * D_BR + D_MODEL]
    p_pool = proj[:, 3 * D_BR + D_MODEL:4 * D_BR + D_MODEL]
    ext[:, HALO:HALO + tm, 0:D_BR] = z.reshape(bb, tm, D_BR)
    ext[:, HALO:HALO + tm, D_BR:D_BR + D_MODEL] = p_rwkv.reshape(bb, tm, D_MODEL)
    ext[:, HALO:HALO + tm, D_BR + D_MODEL:EXT_W] = p_pool.reshape(bb, tm, D_BR)
    cw = convw_ref[...]
    z1 = ext[:, HALO - 1:HALO - 1 + tm, 0:D_BR].reshape(rows, D_BR)
    z2 = ext[:, HALO - 2:HALO - 2 + tm, 0:D_BR].reshape(rows, D_BR)
    ya = bg * (cw[0:1] * z2 + cw[1:2] * z1 + cw[2:3] * z)
    ya_ref[...] = ya.astype(BF16).reshape(bb, tm, D_BR)

    pp = ext[:, HALO - 1:HALO - 1 + tm, D_BR:D_BR + D_MODEL].reshape(rows, D_MODEL)
    xs = p_rwkv + (pp - p_rwkv) * mu_ref[...]
    r = xs[:, 0:D_BR]
    k = xs[:, D_BR:2 * D_BR]
    v = xs[:, 2 * D_BR:3 * D_BR]
    wd = xs[:, 3 * D_BR:3 * D_BR + 64]
    ad = xs[:, 3 * D_BR + 64:3 * D_BR + 128]
    gd = xs[:, 3 * D_BR + 128:D_MODEL]
    w_log = -_softplus(-(w0_ref[...] + _mm(jnp.tanh(wd), ww2_ref[...]))) - 0.5
    logw = -jnp.exp(w_log)
    a = _sigmoid(a0_ref[...] + _mm(ad, wa2_ref[...]))
    g = _mm(_sigmoid(gd), wg2_ref[...])
    hsum = _head_sum_matrix()
    kkr = k * kk_ref[...]
    kk = kkr / jnp.maximum(jnp.sqrt(_mm(kkr * kkr, hsum)), 1e-12)
    kmod = k * (1.0 + (a - 1.0) * ka_ref[...])
    bvec = kk * a
    tok = t * tm + (lax.broadcasted_iota(jnp.int32, (rows, 1), 0) & (tm - 1))
    if t_real < nt * tm:
        live = jnp.where(tok < t_real, 1.0, 0.0)
        logw, kmod, v, kk, bvec = logw * live, kmod * live, v * live, kk * live, bvec * live
    for i, val in enumerate((r, logw, kmod, v, kk, bvec, g)):
        rw_ref[:, :, i * D_BR:(i + 1) * D_BR] = val.reshape(bb, tm, D_BR)

    col = lax.broadcasted_iota(jnp.int32, (1, D_BR), 1)
    acc = p_pool
    sel = None
    for i in range(1, POOL_WINDOWS[-1]):
        acc = acc + ext[:, HALO - i:HALO - i + tm, D_BR + D_MODEL:EXT_W].reshape(rows, D_BR)
        if (i + 1) in POOL_WINDOWS:
            gi = POOL_WINDOWS.index(i + 1)
            sel = acc if sel is None else jnp.where(col >= gi * 64, acc, sel)
    wcol = jnp.where(col < 64, 2, jnp.where(col < 128, 4, jnp.where(col < 192, 8, 16)))
    cnt = jnp.minimum(wcol, pos0 + tok + 1).astype(F32)
    dd = sel / cnt - p_pool
    yc = _mm(dd, poolw_ref[...]) * pools_ref[...]
    yc_ref[...] = yc.astype(BF16).reshape(bb, tm, D_BR)

    q = proj[:, 4 * D_BR + D_MODEL:5 * D_BR + D_MODEL]
    ks = proj[:, 5 * D_BR + D_MODEL:6 * D_BR + D_MODEL]
    vs = proj[:, 6 * D_BR + D_MODEL:D_IN]
    qn = q * lax.rsqrt(_mm(q * q, hsum) * (1.0 / HEAD_DIM) + EPS) * qg_ref[...]
    kn = ks * lax.rsqrt(_mm(ks * ks, hsum) * (1.0 / HEAD_DIM) + EPS) * kg_ref[...]
    q_ref[...] = (qn * HEAD_DIM ** -0.5).astype(BF16).reshape(bb, tm, D_BR)
    k_ref[...] = kn.reshape(bb, tm, D_BR)
    v_ref[...] = vs.reshape(bb, tm, D_BR)

    @pl.when(t == nt - 1)
    def _():
        last = HALO + t_real - (nt - 1) * tm - 1
        conv_ref[...] = ext[:, last - (CONV_W - 2):last + 1, 0:D_BR]
        shift_ref[...] = ext[:, last:last + 1, D_BR:D_BR + D_MODEL]
        pool_ref[...] = ext[:, last - (POOL_CTX - 1):last + 1, D_BR + D_MODEL:EXT_W]

    if nt > 1:
        ext[:, 0:HALO, :] = ext[:, tm:tm + HALO, :]


def _mix_call(x, mod, ng, win, halo0, p, l, bb, tm, t_real, pos0):
    bsz, t, _ = x.shape
    nt = t // tm
    kern = functools.partial(_mix_kernel, tm=tm, t_real=t_real, pos0=pos0, nt=nt)
    tok = lambda w: pl.BlockSpec((bb, tm, w), lambda i, j: (i, j, 0))
    per_b = lambda r, w: pl.BlockSpec((bb, r, w), lambda i, j: (i, 0, 0))
    row = lambda r, w: pl.BlockSpec((None, r, w), lambda i, j: (l, 0, 0))
    sds = lambda shape, dt: jax.ShapeDtypeStruct(shape, dt)
    return pl.pallas_call(
        kern,
        grid=(bsz // bb, nt),
        in_specs=[tok(D_MODEL), per_b(N_MOD, D_MODEL),
                  pl.BlockSpec((None, 1, D_MODEL), lambda i, j: (3 * l + 1, 0, 0)),
                  _resident((None, D_MODEL, D_IN), lambda i, j: (l, 0, 0)),
                  per_b(HALO, EXT_W),
                  row(CONV_W, D_BR), row(1, D_MODEL), row(1, D_BR), row(64, D_BR), row(1, D_BR),
                  row(64, D_BR), row(128, D_BR), row(1, D_BR), row(1, D_BR), row(D_BR, D_BR),
                  row(1, D_BR), row(1, D_BR), row(1, D_BR)],
        out_specs=[tok(D_MODEL), tok(D_BR), tok(D_BR), tok(RW_W), tok(D_BR), tok(D_BR), tok(D_BR),
                   per_b(CONV_W - 1, D_BR), per_b(1, D_MODEL), per_b(POOL_CTX, D_BR)],
        out_shape=[sds((bsz, t, D_MODEL), BF16), sds((bsz, t, D_BR), BF16), sds((bsz, t, D_BR), BF16),
                   sds((bsz, t, RW_W), F32), sds((bsz, t, D_BR), BF16), sds((bsz, t, D_BR), F32),
                   sds((bsz, t, D_BR), F32), sds((bsz, CONV_W - 1, D_BR), F32),
                   sds((bsz, 1, D_MODEL), F32), sds((bsz, POOL_CTX, D_BR), F32)],
        scratch_shapes=[pltpu.VMEM((bb, HALO + tm, EXT_W), F32)],
        compiler_params=_cparams(("parallel", "arbitrary")),
        name="mix_prep",
    )(x, mod, ng, win, halo0, p["conv_w"], p["mu"], p["w0"], p["w2"], p["a0"], p["a2"], p["g2"],
      p["k_k"], p["k_a"], p["pool_bd"], p["pool_scale"], p["q_g"], p["k_g"])


def _rwkv_kernel(rw_ref, s0_ref, rk_ref, lng_ref, lnb_ref, y_ref, sout_ref, s_scr, *, nc, bb):
    c_idx = pl.program_id(1)

    @pl.when(c_idx == 0)
    def _():
        s_scr[...] = s0_ref[...]

    c_in = rw_ref.shape[1]
    csz = RWKV_CHUNK
    ri = lax.broadcasted_iota(jnp.int32, (csz, csz), 0)
    ci = lax.broadcasted_iota(jnp.int32, (csz, csz), 1)
    lower = ri >= ci
    strict = ri > ci
    tril = jnp.where(lower, 1.0, 0.0).astype(BF16)
    eye = jnp.where(ri == ci, 1.0, 0.0)
    levels = [((ri >> (s + 1)) == (ci >> (s + 1))) & ((ri >> s) != (ci >> s))
              for s in range(csz.bit_length() - 1)]

    probs = [(b, h) for b in range(bb) for h in range(N_HEADS)]
    each = lambda fn, *lists: [fn(*args) for args in zip(*lists)]

    rws, cums = [], []
    for b in range(bb):
        rw = rw_ref[b]
        if c_in < csz:
            rw = jnp.concatenate([rw, jnp.zeros((csz - c_in, RW_W), F32)], axis=0)
        lh, ll = _split(rw[:, D_BR:2 * D_BR])
        cums.append(jnp.dot(tril, lh, preferred_element_type=F32)
                    + jnp.dot(tril, ll, preferred_element_type=F32))
        rws.append(rw)

    def operands(b, h):
        lo = h * HEAD_DIM
        r, logw, k, v, kk, bv, g = (rws[b][:, i * D_BR + lo:i * D_BR + lo + HEAD_DIM] for i in range(7))
        cum = cums[b][:, lo:lo + HEAD_DIM]
        tot = cum[csz - 1:csz, :]
        e_inv = jnp.exp(-cum)
        e_rest = jnp.exp(tot - cum)
        return dict(r=r, k=k, v=v, g=g, kt=kk * jnp.exp(cum - logw), bt=bv * e_inv, kd=k * e_inv,
                    rt=r * jnp.exp(cum), bh=bv * e_rest, kh=k * e_rest, gam=jnp.exp(tot))

    ops = [operands(b, h) for b, h in probs]
    get = lambda name: [o[name] for o in ops]
    kt, bt, kd, rt, v = get("kt"), get("bt"), get("kd"), get("rt"), get("v")

    kr = each(lambda a, c: jnp.concatenate([a, c], axis=0), kt, rt)
    pb = each(_mm_nt, kr, bt)
    pk = each(_mm_nt, kr, kd)
    a_b = [jnp.where(strict, p[0:csz], 0.0) for p in pb]
    m_b = [jnp.where(lower, p[csz:2 * csz], 0.0) for p in pb]
    a_k = [jnp.where(strict, p[0:csz], 0.0) for p in pk]
    m_k = [jnp.where(lower, p[csz:2 * csz], 0.0) for p in pk]

    tinv = [eye - jnp.where(levels[0], a, 0.0) for a in a_b]
    for lv in levels[1:]:
        x = each(lambda a, t: _mm(jnp.where(lv, a, 0.0), t), a_b, tinv)
        tinv = each(lambda t, xx: t - _mm(t, xx), tinv, x)

    akv = each(_mm, a_k, v)
    w1 = each(_mm, tinv, kt)
    w2 = each(_mm, tinv, akv)
    qt = each(lambda r_, m, w: r_ - _mm(m, w), rt, m_b, w1)
    y0 = each(lambda mk, vv, mb, w: _mm(mk, vv) - _mm(mb, w), m_k, v, m_b, w2)
    m_neg = each(_mm_tn, w1, get("bh"))
    n_add = each(lambda vv, w, kh_, bh_: _mm_tn(jnp.concatenate([vv, w], axis=0),
                                                jnp.concatenate([kh_, -bh_], axis=0)),
                 v, w2, get("kh"), get("bh"))

    s_prev = [s_scr[b, h] for b, h in probs]
    y = each(lambda q_, s_, y_: _mm_nt(q_, s_) + y_, qt, s_prev, y0)
    s_new = each(lambda s_, o, mn, na: s_ * o["gam"] - _mm(s_, mn) + na, s_prev, ops, m_neg, n_add)
    for (b, h), s_ in zip(probs, s_new):
        s_scr[b, h] = s_

    for (b, h), o, y_ in zip(probs, ops, y):
        sl = slice(h * HEAD_DIM, (h + 1) * HEAD_DIM)
        mean = jnp.mean(y_, axis=-1, keepdims=True)
        yc = y_ - mean
        var = jnp.mean(yc * yc, axis=-1, keepdims=True)
        yn = yc * lax.rsqrt(var + LNX_EPS) * lng_ref[:, sl] + lnb_ref[:, sl]
        bonus = jnp.sum(o["r"] * o["k"] * rk_ref[:, sl], axis=-1, keepdims=True) * o["v"]
        y_ref[b, :, sl] = ((yn + bonus) * o["g"])[0:c_in].astype(BF16)

    @pl.when(c_idx == nc - 1)
    def _():
        sout_ref[...] = s_scr[...]


def _rwkv_call(rw, s0, p, l, csz, bb):
    bsz, t, _ = rw.shape
    nc = t // csz
    row = lambda: pl.BlockSpec((None, 1, D_BR), lambda i, j: (l, 0, 0))
    st = pl.BlockSpec((bb, N_HEADS, HEAD_DIM, HEAD_DIM), lambda i, j: (i, 0, 0, 0))
    return pl.pallas_call(
        functools.partial(_rwkv_kernel, nc=nc, bb=bb),
        grid=(bsz // bb, nc),
        in_specs=[pl.BlockSpec((bb, csz, RW_W), lambda i, j: (i, j, 0)), st, row(), row(), row()],
        out_specs=[pl.BlockSpec((bb, csz, D_BR), lambda i, j: (i, j, 0)), st],
        out_shape=[jax.ShapeDtypeStruct((bsz, t, D_BR), BF16),
                   jax.ShapeDtypeStruct((bsz, N_HEADS, HEAD_DIM, HEAD_DIM), F32)],
        scratch_shapes=[pltpu.VMEM((bb, N_HEADS, HEAD_DIM, HEAD_DIM), F32)],
        compiler_params=_cparams(("parallel", "arbitrary")),
        name="rwkv",
    )(rw, s0, p["r_k"], p["lnx_g"], p["lnx_b"])


def _sb_block(z, causal, tri, carry):
    sp = _softplus(z)
    lk = -sp
    if causal is not None:
        lk = jnp.where(causal, lk, 0.0)
    after = jnp.dot(lk.astype(BF16), tri, preferred_element_type=F32)
    a = jnp.exp(z - sp + after + carry)
    if causal is not None:
        a = jnp.where(causal, a, 0.0)
    return a, carry + after[:, 0:1] + lk[:, 0:1]


def _sba_prompt_kernel(bias_ref, q_ref, k_ref, v_ref, tri_ref, o_ref, acc, carry, *, nk, tq):
    qi = pl.program_id(1)
    j = pl.program_id(2)

    @pl.when(j == 0)
    def _():
        acc[...] = jnp.zeros_like(acc)
        carry[...] = jnp.zeros_like(carry)

    def compute(diag):
        q = q_ref[0]
        kb = k_ref[0].astype(BF16)
        vb = v_ref[0].astype(BF16)
        tri = tri_ref[...]
        causal = None
        if diag:
            causal = (lax.broadcasted_iota(jnp.int32, (tq, tq), 1)
                      < lax.broadcasted_iota(jnp.int32, (tq, tq), 0))
        for h in range(N_HEADS):
            sl = slice(h * HEAD_DIM, (h + 1) * HEAD_DIM)
            z = lax.dot_general(q[:, sl], kb[:, sl], (((1,), (1,)), ((), ())),
                                preferred_element_type=F32) + bias_ref[h]
            a, c_new = _sb_block(z, causal, tri, carry[h])
            carry[h] = c_new
            acc[:, sl] += jnp.dot(a.astype(BF16), vb[:, sl], preferred_element_type=F32)

    @pl.when(j == 0)
    def _():
        compute(True)

    @pl.when((j > 0) & (j <= qi))
    def _():
        compute(False)

    @pl.when(j == nk - 1)
    def _():
        o_ref[0] = acc[...].astype(BF16)


def _sba_prompt_call(q, k, v, bias, tri, tq):
    bsz, t, _ = q.shape
    nq = t // tq
    kv_map = lambda b, i, j: (b, jnp.maximum(i - j, 0), 0)
    return pl.pallas_call(
        functools.partial(_sba_prompt_kernel, nk=nq, tq=tq),
        grid=(bsz, nq, nq),
        in_specs=[pl.BlockSpec(memory_space=pltpu.SMEM),
                  pl.BlockSpec((1, tq, D_BR), lambda b, i, j: (b, i, 0)),
                  pl.BlockSpec((1, tq, D_BR), kv_map),
                  pl.BlockSpec((1, tq, D_BR), kv_map),
                  _resident((tq, tq), lambda b, i, j: (0, 0))],
        out_specs=pl.BlockSpec((1, tq, D_BR), lambda b, i, j: (b, i, 0)),
        out_shape=jax.ShapeDtypeStruct((bsz, t, D_BR), BF16),
        scratch_shapes=[pltpu.VMEM((tq, D_BR), F32), pltpu.VMEM((N_HEADS, tq, 1), F32)],
        compiler_params=_cparams(("parallel", "parallel", "arbitrary")),
        name="sba_prompt",
    )(bias, q, k, v, tri)


def _sba_sample_kernel(pt_ref, bias_ref, q_ref, kn_ref, vn_ref, tri_ref, *rest, npg, tpad, nsteps):
    k_refs = rest[:npg]
    v_refs = rest[npg:2 * npg]
    o_ref = rest[2 * npg]
    acc, carry = rest[2 * npg + 1:]
    s = pl.program_id(1)
    rows = N_HEADS * tpad
    tri = tri_ref[...]
    tshift = tpad.bit_length() - 1
    rid = lax.broadcasted_iota(jnp.int32, (rows, D_BR), 0) >> tshift
    cid = lax.broadcasted_iota(jnp.int32, (rows, D_BR), 1) >> 6
    head_match = rid == cid
    q = q_ref[0].astype(F32)
    q_bd = jnp.where(head_match, jnp.concatenate([q] * N_HEADS, axis=0), 0.0).astype(BF16)
    hrow = lax.broadcasted_iota(jnp.int32, (rows, 1), 0) >> tshift
    bias = jnp.zeros((rows, 1), F32)
    for h in range(N_HEADS):
        bias = jnp.where(hrow == h, bias_ref[h], bias)

    @pl.when(s == 0)
    def _():
        kpos = lax.broadcasted_iota(jnp.int32, (rows, PAGE), 1)
        qpos = lax.broadcasted_iota(jnp.int32, (rows, PAGE), 0) & (tpad - 1)
        z = lax.dot_general(q_bd, kn_ref[0].astype(BF16), (((1,), (1,)), ((), ())),
                            preferred_element_type=F32) + bias
        a, c_new = _sb_block(z, kpos < qpos, tri, jnp.zeros((rows, 1), F32))
        carry[...] = c_new
        acc[...] = jnp.dot(a.astype(BF16), vn_ref[0].astype(BF16), preferred_element_type=F32)

    kt = jnp.concatenate([r[0, 0] for r in k_refs], axis=1).astype(BF16)
    vt = jnp.concatenate([r[0, 0] for r in v_refs], axis=1).astype(BF16)
    z = jnp.dot(q_bd, kt, preferred_element_type=F32) + bias
    sp = _softplus(z)
    lk = -sp
    page = lambda x, i: x[:, i * PAGE:(i + 1) * PAGE]
    local = [jnp.dot(page(lk, i).astype(BF16), tri, preferred_element_type=F32) for i in range(npg)]
    totals = [local[i][:, 0:1] + page(lk, i)[:, 0:1] for i in range(npg)]
    c = carry[...]
    after = [None] * npg
    for i in range(npg - 1, -1, -1):
        after[i] = local[i] + c
        c = c + totals[i]
    carry[...] = c
    a = jnp.exp(z - sp + jnp.concatenate(after, axis=1))
    acc[...] += lax.dot_general(a.astype(BF16), vt, (((1,), (1,)), ((), ())), preferred_element_type=F32)

    @pl.when(s == nsteps - 1)
    def _():
        masked = jnp.where(head_match, acc[...], 0.0)
        out = masked[0:tpad]
        for h in range(1, N_HEADS):
            out = out + masked[h * tpad:(h + 1) * tpad]
        o_ref[0] = out.astype(BF16)


def _sba_sample_call(q, kn, vn, cache_k, cache_v, page_table, bias, tri, l, npg):
    bsz, tpad, _ = q.shape
    n_pages = page_table.shape[1]
    nsteps = n_pages // npg
    rows = N_HEADS * tpad

    def page_spec(i):
        return pl.BlockSpec((1, 1, D_BR, PAGE),
                            lambda b, s, pt: (l, pt[b, n_pages - (s + 1) * npg + i], 0, 0))

    grid_spec = pltpu.PrefetchScalarGridSpec(
        num_scalar_prefetch=1,
        grid=(bsz, nsteps),
        in_specs=[pl.BlockSpec(memory_space=pltpu.SMEM),
                  pl.BlockSpec((1, tpad, D_BR), lambda b, s, pt: (b, 0, 0)),
                  pl.BlockSpec((1, PAGE, D_BR), lambda b, s, pt: (b, 0, 0)),
                  pl.BlockSpec((1, PAGE, D_BR), lambda b, s, pt: (b, 0, 0)),
                  pl.BlockSpec((PAGE, PAGE), lambda b, s, pt: (0, 0))]
        + [page_spec(i) for i in range(npg)] + [page_spec(i) for i in range(npg)],
        out_specs=pl.BlockSpec((1, tpad, D_BR), lambda b, s, pt: (b, 0, 0)),
        scratch_shapes=[pltpu.VMEM((rows, D_BR), F32), pltpu.VMEM((rows, 1), F32)],
    )
    return pl.pallas_call(
        functools.partial(_sba_sample_kernel, npg=npg, tpad=tpad, nsteps=nsteps),
        grid_spec=grid_spec,
        out_shape=jax.ShapeDtypeStruct((bsz, tpad, D_BR), BF16),
        compiler_params=_cparams(("parallel", "arbitrary")),
        name="sba_sample",
    )(page_table, bias, q, kn, vn, tri, *([cache_k] * npg), *([cache_v] * npg))


def _merge_kernel(x_ref, mod_ref, u_ref, ya_ref, yb_ref, yc_ref, yd_ref, wg_ref, wb_ref, wo_ref, o_ref):
    x = x_ref[...]
    bb, tm, _ = x.shape
    rows = bb * tm
    ub = u_ref[...].reshape(rows, D_MODEL)
    merged = jnp.zeros((rows, D_MODEL), F32)
    for i, y_ref in enumerate((ya_ref, yb_ref, yc_ref, yd_ref)):
        gate = _sigmoid(jnp.dot(ub, wg_ref[i], preferred_element_type=F32))
        br = jnp.dot(y_ref[...].reshape(rows, D_BR), wb_ref[i], preferred_element_type=F32)
        merged = merged + gate * br
    out = jnp.dot(merged.astype(BF16), wo_ref[...], preferred_element_type=F32)
    g2 = mod_ref[:, 5:6, :]
    o_ref[...] = x + g2 * out.reshape(bb, tm, D_MODEL)


def _merge_call(x, mod, ub, ya, yb, yc, yd, wg, wb, wo, l, bb, tm):
    bsz, t, _ = x.shape
    tok = lambda w: pl.BlockSpec((bb, tm, w), lambda i, j: (i, j, 0))
    return pl.pallas_call(
        _merge_kernel,
        grid=(bsz // bb, t // tm),
        in_specs=[tok(D_MODEL), pl.BlockSpec((bb, N_MOD, D_MODEL), lambda i, j: (i, 0, 0)),
                  tok(D_MODEL), tok(D_BR), tok(D_BR), tok(D_BR), tok(D_BR),
                  _resident((None, N_BRANCH, D_MODEL, D_MODEL), lambda i, j: (l, 0, 0, 0)),
                  _resident((None, N_BRANCH, D_BR, D_MODEL), lambda i, j: (l, 0, 0, 0)),
                  _resident((None, D_MODEL, D_MODEL), lambda i, j: (l, 0, 0))],
        out_specs=tok(D_MODEL),
        out_shape=jax.ShapeDtypeStruct(x.shape, F32),
        compiler_params=_cparams(("parallel", "parallel")),
        name="merge",
    )(x, mod, ub, ya, yb, yc, yd, wg, wb, wo)


def _tri_matrix(n):
    r = lax.broadcasted_iota(jnp.int32, (n, n), 0)
    c = lax.broadcasted_iota(jnp.int32, (n, n), 1)
    return jnp.where(r > c, 1.0, 0.0).astype(BF16)


def kernel(x_prompt, x_sample, c_prompt, c_sample, cache_k, cache_v, page_table, state_conv, state_shift, state_wkv, state_pool, ada_w, ada_b, norm_g, w_ffn_up, w_ffn_down, w_in, conv_w, rwkv_mu, rwkv_w0, rwkv_w2, rwkv_a0, rwkv_a2, rwkv_g2, rwkv_k_k, rwkv_k_a, rwkv_r_k, rwkv_lnx_g, rwkv_lnx_b, pool_w, pool_scale, q_norm_g, k_norm_g, sb_bias, w_gate, w_branch, w_out):
    depth = ada_w.shape[0]
    bp, seq, _ = x_prompt.shape
    bs, dec_seq, _ = x_sample.shape
    n_phys = cache_k.shape[1]
    past_len = page_table.shape[1] * PAGE
    tpad = 8
    tm_p = 512
    tq_p = 512

    wup = w_ffn_up.astype(BF16)
    wdn = w_ffn_down.astype(BF16)
    win = w_in.astype(BF16)
    wg = w_gate.astype(BF16)
    wb = w_branch.astype(BF16)
    wo = w_out.astype(BF16)
    ng = norm_g.reshape(depth * 3, 1, D_MODEL)
    pool_bd = jnp.zeros((depth, len(POOL_WINDOWS), 64, len(POOL_WINDOWS), 64), F32)
    for gi in range(len(POOL_WINDOWS)):
        pool_bd = pool_bd.at[:, gi, :, gi, :].set(pool_w[:, gi])
    prm = {
        "conv_w": conv_w, "mu": rwkv_mu[:, None, :], "w0": rwkv_w0[:, None, :], "w2": rwkv_w2,
        "a0": rwkv_a0[:, None, :], "a2": rwkv_a2, "g2": rwkv_g2, "k_k": rwkv_k_k[:, None, :],
        "k_a": rwkv_k_a[:, None, :], "pool_bd": pool_bd.reshape(depth, D_BR, D_BR).astype(BF16),
        "pool_scale": pool_scale[:, None, :],
        "q_g": jnp.tile(q_norm_g, (1, N_HEADS))[:, None, :],
        "k_g": jnp.tile(k_norm_g, (1, N_HEADS))[:, None, :],
        "r_k": rwkv_r_k[:, None, :], "lnx_g": rwkv_lnx_g[:, None, :], "lnx_b": rwkv_lnx_b[:, None, :],
    }
    ck = jnp.transpose(cache_k, (0, 1, 3, 4, 2)).reshape(depth, n_phys, D_BR, PAGE)
    cv = jnp.transpose(cache_v, (0, 1, 3, 4, 2)).reshape(depth, n_phys, D_BR, PAGE)
    tri_p = _tri_matrix(tq_p)
    tri_s = _tri_matrix(PAGE)

    n_c = bp + bs
    c_all = jnp.concatenate([c_prompt, c_sample, jnp.zeros((-n_c % 8, D_MODEL), F32)], axis=0)
    mod_all = _ada_call(c_all, ada_w, ada_b)
    xs_pad = jnp.concatenate([x_sample, jnp.zeros((bs, tpad - dec_seq, D_MODEL), F32)], axis=1)

    halo_p = jnp.zeros((bp, HALO, EXT_W), F32)
    wkv0_p = jnp.zeros((bp, N_HEADS, HEAD_DIM, HEAD_DIM), F32)

    hp, hs = x_prompt, xs_pad
    outs_p = [[] for _ in range(6)]
    outs_s = [[] for _ in range(6)]
    for l in range(depth):
        mod_p = mod_all[l, :bp].reshape(bp, N_MOD, D_MODEL)
        mod_s = mod_all[l, bp:n_c].reshape(bs, N_MOD, D_MODEL)
        halo_s = jnp.zeros((bs, HALO, EXT_W), F32)
        halo_s = halo_s.at[:, HALO - (CONV_W - 1):, 0:D_BR].set(state_conv[l])
        halo_s = halo_s.at[:, HALO - 1, D_BR:D_BR + D_MODEL].set(state_shift[l])
        halo_s = halo_s.at[:, HALO - POOL_CTX:, D_BR + D_MODEL:].set(state_pool[l])

        for grp in ("p", "s"):
            if grp == "p":
                x, mod, bb, tm, t_real, pos0, halo0, wkv0 = hp, mod_p, 1, tm_p, seq, 0, halo_p, wkv0_p
            else:
                x, mod, bb, tm, t_real, pos0, halo0, wkv0 = hs, mod_s, bs, tpad, dec_seq, past_len, halo_s, state_wkv[l]
            x = _ffn_call(x, mod, ng, wup, wdn, l, 0, bb, tm)
            ub, ya, yc, rw, q, k, v, conv_n, shift_n, pool_n = _mix_call(
                x, mod, ng, win, halo0, prm, l, bb, tm, t_real, pos0)
            seqs = max(d for d in (1, 2, 4) if rw.shape[0] % d == 0)
            yb, wkv_n = _rwkv_call(rw, wkv0, prm, l, RWKV_CHUNK if grp == "p" else tpad, seqs)
            if grp == "p":
                yd = _sba_prompt_call(q, k, v, sb_bias[l], tri_p, tq_p)
            else:
                kn = jnp.concatenate([k, jnp.zeros((bs, PAGE - tpad, D_BR), F32)], axis=1)
                vn = jnp.concatenate([v, jnp.zeros((bs, PAGE - tpad, D_BR), F32)], axis=1)
                yd = _sba_sample_call(q, kn, vn, ck, cv, page_table, sb_bias[l], tri_s, l,
                                      16 if page_table.shape[1] % 16 == 0 else 8)
            x = _merge_call(x, mod, ub, ya, yb, yc, yd, wg, wb, wo, l, bb, tm)
            x = _ffn_call(x, mod, ng, wup, wdn, l, 1, bb, tm)
            new = (k[:, :t_real].reshape(-1, t_real, N_HEADS, HEAD_DIM),
                   v[:, :t_real].reshape(-1, t_real, N_HEADS, HEAD_DIM),
                   conv_n, shift_n[:, 0], wkv_n, pool_n)
            if grp == "p":
                hp = x
                for lst, a in zip(outs_p, new):
                    lst.append(a)
            else:
                hs = x
                for lst, a in zip(outs_s, new):
                    lst.append(a)

    stacked_p = [jnp.stack(a, axis=0) for a in outs_p]
    stacked_s = [jnp.stack(a, axis=0) for a in outs_s]
    return (hp, hs[:, :dec_seq], *stacked_p, *stacked_s)
```

```python
import functools

import jax
import jax.numpy as jnp
from jax import lax
from jax.experimental import pallas as pl
from jax.experimental.pallas import tpu as pltpu

F32 = jnp.float32
BF16 = jnp.bfloat16

D_MODEL = 1024
N_BRANCH = 4
D_BR = 256
HEAD_DIM = 64
N_HEADS = 4
CONV_W = 3
POOL_WINDOWS = (2, 4, 8, 16)
POOL_CTX = 15
D_FF = 2816
D_IN = 2816
N_MOD = 9
PAGE = 128
EPS = 1e-6
LNX_EPS = 64e-5

HALO = 16
EXT_W = D_BR + D_MODEL + D_BR
RW_W = 7 * D_BR
RWKV_CHUNK = 64
VMEM_LIMIT = 56 * 1024 * 1024


def _cparams(sem):
    return pltpu.CompilerParams(dimension_semantics=sem, vmem_limit_bytes=VMEM_LIMIT)


def _resident(shape, index_map):
    return pl.BlockSpec(shape, index_map, pipeline_mode=pl.Buffered(1))


def _mm(a, b):
    return jnp.dot(a.astype(BF16), b.astype(BF16), preferred_element_type=F32)


def _mm_nt(a, b):
    return lax.dot_general(a.astype(BF16), b.astype(BF16), (((1,), (1,)), ((), ())),
                           preferred_element_type=F32)


def _mm_tn(a, b):
    return lax.dot_general(a.astype(BF16), b.astype(BF16), (((0,), (0,)), ((), ())),
                           preferred_element_type=F32)


def _split(a):
    hi = a.astype(BF16)
    lo = (a - hi.astype(F32)).astype(BF16)
    return hi, lo


def _mm3(a, b):
    ah, al = _split(a)
    bh, bl = _split(b)
    out = jnp.dot(ah, bh, preferred_element_type=F32)
    out = out + jnp.dot(ah, bl, preferred_element_type=F32)
    return out + jnp.dot(al, bh, preferred_element_type=F32)


def _softplus(y):
    return jnp.maximum(y, 0.0) + jnp.log(1.0 + jnp.exp(-jnp.abs(y)))


def _sigmoid(y):
    return 1.0 / (1.0 + jnp.exp(-y))


def _neg_abs(y):
    bits = lax.bitcast_convert_type(y, jnp.uint32) | jnp.uint32(0x80000000)
    return lax.bitcast_convert_type(bits, F32)


def _log_sigmoids(z):
    ls = jnp.minimum(z, 0.0) - jnp.log(1.0 + jnp.exp(_neg_abs(z)))
    return ls, ls - z


def _head_sum_matrix():
    r = lax.broadcasted_iota(jnp.int32, (D_BR, D_BR), 0) >> 6
    c = lax.broadcasted_iota(jnp.int32, (D_BR, D_BR), 1) >> 6
    return jnp.where(r == c, 1.0, 0.0).astype(BF16)


def _modulated_norm(x, mod_ref, ng_ref, mi):
    sh = mod_ref[:, 3 * mi:3 * mi + 1, :]
    sc = mod_ref[:, 3 * mi + 1:3 * mi + 2, :]
    ms = jnp.mean(x * x, axis=-1, keepdims=True)
    u = x * lax.rsqrt(ms + EPS) * ng_ref[...]
    return u * (1.0 + sc) + sh


def _ada_kernel(c_ref, w_ref, b_ref, o_ref):
    c = c_ref[...]
    s = c * _sigmoid(c)
    o_ref[0] = _mm3(s, w_ref[0]) + b_ref[0]


def _ada_call(c_all, ada_w, ada_b):
    depth = ada_w.shape[0]
    rows = c_all.shape[0]
    n_out = ada_w.shape[2]
    tn = 1152
    return pl.pallas_call(
        _ada_kernel,
        grid=(depth, n_out // tn),
        in_specs=[pl.BlockSpec((rows, D_MODEL), lambda l, j: (0, 0)),
                  pl.BlockSpec((1, D_MODEL, tn), lambda l, j: (l, 0, j)),
                  pl.BlockSpec((1, 1, tn), lambda l, j: (l, 0, j))],
        out_specs=pl.BlockSpec((1, rows, tn), lambda l, j: (l, 0, j)),
        out_shape=jax.ShapeDtypeStruct((depth, rows, n_out), F32),
        compiler_params=_cparams(("parallel", "parallel")),
        name="ada_mod",
    )(c_all, ada_w, ada_b.reshape(depth, 1, n_out))


def _ffn_kernel(x_ref, mod_ref, ng_ref, wup_ref, wdn_ref, o_ref, *, mi, nf):
    x = x_ref[...]
    bb, tm, _ = x.shape
    u = _modulated_norm(x, mod_ref, ng_ref, mi)
    ub = u.reshape(bb * tm, D_MODEL).astype(BF16)
    tf = D_FF // nf
    acc = jnp.zeros((bb * tm, D_MODEL), F32)
    for j in range(nf):
        a = jnp.dot(ub, wup_ref[:, j * tf:(j + 1) * tf], preferred_element_type=F32)
        b = jnp.dot(ub, wup_ref[:, D_FF + j * tf:D_FF + (j + 1) * tf], preferred_element_type=F32)
        h = (a * _sigmoid(a) * b).astype(BF16)
        acc = acc + jnp.dot(h, wdn_ref[j * tf:(j + 1) * tf, :], preferred_element_type=F32)
    g = mod_ref[:, 3 * mi + 2:3 * mi + 3, :]
    o_ref[...] = x + 0.5 * g * acc.reshape(bb, tm, D_MODEL)


def _ffn_call(x, mod, ng, wup, wdn, l, which, bb, tm):
    bsz, t, _ = x.shape
    mi = 0 if which == 0 else 2
    kern = functools.partial(_ffn_kernel, mi=mi, nf=2)
    return pl.pallas_call(
        kern,
        grid=(bsz // bb, t // tm),
        in_specs=[pl.BlockSpec((bb, tm, D_MODEL), lambda i, j: (i, j, 0)),
                  pl.BlockSpec((bb, N_MOD, D_MODEL), lambda i, j: (i, 0, 0)),
                  pl.BlockSpec((None, 1, D_MODEL), lambda i, j: (3 * l + mi, 0, 0)),
                  _resident((None, None, D_MODEL, 2 * D_FF), lambda i, j: (l, which, 0, 0)),
                  _resident((None, None, D_FF, D_MODEL), lambda i, j: (l, which, 0, 0))],
        out_specs=pl.BlockSpec((bb, tm, D_MODEL), lambda i, j: (i, j, 0)),
        out_shape=jax.ShapeDtypeStruct(x.shape, F32),
        compiler_params=_cparams(("parallel", "parallel")),
        name="ffn",
    )(x, mod, ng, wup, wdn)


def _mix_kernel(x_ref, mod_ref, ng_ref, win_ref, halo_ref, convw_ref, mu_ref, w0_ref, ww2_ref,
                a0_ref, wa2_ref, wg2_ref, kk_ref, ka_ref, poolw_ref, pools_ref, qg_ref, kg_ref,
                u_ref, ya_ref, yc_ref, rw_ref, q_ref, k_ref, v_ref, kb_ref, vb_ref,
                conv_ref, shift_ref, pool_ref, ext, *, tm, t_real, pos0, nt):
    t = pl.program_id(1)

    @pl.when(t == 0)
    def _():
        ext[:, 0:HALO, :] = halo_ref[...]

    x = x_ref[...]
    bb = x.shape[0]
    rows = bb * tm
    u = _modulated_norm(x, mod_ref, ng_ref, 1)
    ub = u.reshape(rows, D_MODEL).astype(BF16)
    u_ref[...] = ub.reshape(bb, tm, D_MODEL)
    proj = jnp.dot(ub, win_ref[...], preferred_element_type=F32)

    bg = proj[:, 0:D_BR]
    z = proj[:, D_BR:2 * D_BR] * proj[:, 2 * D_BR:3 * D_BR]
    p_rwkv = proj[:, 3 * D_BR:3 * D_BR + D_MODEL]
    p_pool = proj[:, 3 * D_BR + D_MODEL:4 * D_BR + D_MODEL]
    ext[:, HALO:HALO + tm, 0:D_BR] = z.reshape(bb, tm, D_BR)
    ext[:, HALO:HALO + tm, D_BR:D_BR + D_MODEL] = p_rwkv.reshape(bb, tm, D_MODEL)
    ext[:, HALO:HALO + tm, D_BR + D_MODEL:EXT_W] = p_pool.reshape(bb, tm, D_BR)
    cw = convw_ref[...]
    z1 = ext[:, HALO - 1:HALO - 1 + tm, 0:D_BR].reshape(rows, D_BR)
    z2 = ext[:, HALO - 2:HALO - 2 + tm, 0:D_BR].reshape(rows, D_BR)
    ya = bg * (cw[0:1] * z2 + cw[1:2] * z1 + cw[2:3] * z)
    ya_ref[...] = ya.astype(BF16).reshape(bb, tm, D_BR)

    pp = ext[:, HALO - 1:HALO - 1 + tm, D_BR:D_BR + D_MODEL].reshape(rows, D_MODEL)
    xs = p_rwkv + (pp - p_rwkv) * mu_ref[...]
    r = xs[:, 0:D_BR]
    k = xs[:, D_BR:2 * D_BR]
    v = xs[:, 2 * D_BR:3 * D_BR]
    wd = xs[:, 3 * D_BR:3 * D_BR + 64]
    ad = xs[:, 3 * D_BR + 64:3 * D_BR + 128]
    gd = xs[:, 3 * D_BR + 128:D_MODEL]
    w_log = -_softplus(-(w0_ref[...] + _mm(jnp.tanh(wd), ww2_ref[...]))) - 0.5
    logw = -jnp.exp(w_log)
    a = _sigmoid(a0_ref[...] + _mm(ad, wa2_ref[...]))
    g = _mm(_sigmoid(gd), wg2_ref[...])
    hsum = _head_sum_matrix()
    kkr = k * kk_ref[...]
    kk = kkr / jnp.maximum(jnp.sqrt(_mm(kkr * kkr, hsum)), 1e-12)
    kmod = k * (1.0 + (a - 1.0) * ka_ref[...])
    bvec = kk * a
    tok = t * tm + (lax.broadcasted_iota(jnp.int32, (rows, 1), 0) & (tm - 1))
    if t_real < nt * tm:
        live = jnp.where(tok < t_real, 1.0, 0.0)
        logw, kmod, v, kk, bvec = logw * live, kmod * live, v * live, kk * live, bvec * live
    for i, val in enumerate((r, logw, kmod, v, kk, bvec, g)):
        rw_ref[:, :, i * D_BR:(i + 1) * D_BR] = val.reshape(bb, tm, D_BR)

    col = lax.broadcasted_iota(jnp.int32, (1, D_BR), 1)
    acc = p_pool
    sel = None
    for i in range(1, POOL_WINDOWS[-1]):
        acc = acc + ext[:, HALO - i:HALO - i + tm, D_BR + D_MODEL:EXT_W].reshape(rows, D_BR)
        if (i + 1) in POOL_WINDOWS:
            gi = POOL_WINDOWS.index(i + 1)
            sel = acc if sel is None else jnp.where(col >= gi * 64, acc, sel)
    wcol = jnp.where(col < 64, 2, jnp.where(col < 128, 4, jnp.where(col < 192, 8, 16)))
    cnt = jnp.minimum(wcol, pos0 + tok + 1).astype(F32)
    dd = sel / cnt - p_pool
    yc = _mm(dd, poolw_ref[...]) * pools_ref[...]
    yc_ref[...] = yc.astype(BF16).reshape(bb, tm, D_BR)

    q = proj[:, 4 * D_BR + D_MODEL:5 * D_BR + D_MODEL]
    ks = proj[:, 5 * D_BR + D_MODEL:6 * D_BR + D_MODEL]
    vs = proj[:, 6 * D_BR + D_MODEL:D_IN]
    qn = q * lax.rsqrt(_mm(q * q, hsum) * (1.0 / HEAD_DIM) + EPS) * qg_ref[...]
    kn = ks * lax.rsqrt(_mm(ks * ks, hsum) * (1.0 / HEAD_DIM) + EPS) * kg_ref[...]
    q_ref[...] = (qn * HEAD_DIM ** -0.5).astype(BF16).reshape(bb, tm, D_BR)
    k_ref[...] = kn.reshape(bb, tm, D_BR)
    v_ref[...] = vs.reshape(bb, tm, D_BR)
    kb_ref[...] = kn.astype(BF16).reshape(bb, tm, D_BR)
    vb_ref[...] = vs.astype(BF16).reshape(bb, tm, D_BR)

    @pl.when(t == nt - 1)
    def _():
        last = HALO + t_real - (nt - 1) * tm - 1
        conv_ref[...] = ext[:, last - (CONV_W - 2):last + 1, 0:D_BR]
        shift_ref[...] = ext[:, last:last + 1, D_BR:D_BR + D_MODEL]
        pool_ref[...] = ext[:, last - (POOL_CTX - 1):last + 1, D_BR + D_MODEL:EXT_W]

    if nt > 1:
        ext[:, 0:HALO, :] = ext[:, tm:tm + HALO, :]


def _mix_call(x, mod, ng, win, halo0, p, l, bb, tm, t_real, pos0):
    bsz, t, _ = x.shape
    nt = t // tm
    kern = functools.partial(_mix_kernel, tm=tm, t_real=t_real, pos0=pos0, nt=nt)
    tok = lambda w: pl.BlockSpec((bb, tm, w), lambda i, j: (i, j, 0))
    per_b = lambda r, w: pl.BlockSpec((bb, r, w), lambda i, j: (i, 0, 0))
    row = lambda r, w: pl.BlockSpec((None, r, w), lambda i, j: (l, 0, 0))
    sds = lambda shape, dt: jax.ShapeDtypeStruct(shape, dt)
    return pl.pallas_call(
        kern,
        grid=(bsz // bb, nt),
        in_specs=[tok(D_MODEL), per_b(N_MOD, D_MODEL),
                  pl.BlockSpec((None, 1, D_MODEL), lambda i, j: (3 * l + 1, 0, 0)),
                  _resident((None, D_MODEL, D_IN), lambda i, j: (l, 0, 0)),
                  per_b(HALO, EXT_W),
                  row(CONV_W, D_BR), row(1, D_MODEL), row(1, D_BR), row(64, D_BR), row(1, D_BR),
                  row(64, D_BR), row(128, D_BR), row(1, D_BR), row(1, D_BR), row(D_BR, D_BR),
                  row(1, D_BR), row(1, D_BR), row(1, D_BR)],
        out_specs=[tok(D_MODEL), tok(D_BR), tok(D_BR), tok(RW_W), tok(D_BR), tok(D_BR), tok(D_BR),
                   tok(D_BR), tok(D_BR),
                   per_b(CONV_W - 1, D_BR), per_b(1, D_MODEL), per_b(POOL_CTX, D_BR)],
        out_shape=[sds((bsz, t, D_MODEL), BF16), sds((bsz, t, D_BR), BF16), sds((bsz, t, D_BR), BF16),
                   sds((bsz, t, RW_W), F32), sds((bsz, t, D_BR), BF16), sds((bsz, t, D_BR), F32),
                   sds((bsz, t, D_BR), F32), sds((bsz, t, D_BR), BF16), sds((bsz, t, D_BR), BF16),
                   sds((bsz, CONV_W - 1, D_BR), F32),
                   sds((bsz, 1, D_MODEL), F32), sds((bsz, POOL_CTX, D_BR), F32)],
        scratch_shapes=[pltpu.VMEM((bb, HALO + tm, EXT_W), F32)],
        compiler_params=_cparams(("parallel", "arbitrary")),
        name="mix_prep",
    )(x, mod, ng, win, halo0, p["conv_w"], p["mu"], p["w0"], p["w2"], p["a0"], p["a2"], p["g2"],
      p["k_k"], p["k_a"], p["pool_bd"], p["pool_scale"], p["q_g"], p["k_g"])


def _rwkv_kernel(rw_ref, s0_ref, rk_ref, lng_ref, lnb_ref, y_ref, sout_ref, s_scr, *, nc, bb):
    c_idx = pl.program_id(1)

    @pl.when(c_idx == 0)
    def _():
        s_scr[...] = s0_ref[...]

    c_in = rw_ref.shape[1]
    csz = RWKV_CHUNK
    ri = lax.broadcasted_iota(jnp.int32, (csz, csz), 0)
    ci = lax.broadcasted_iota(jnp.int32, (csz, csz), 1)
    lower = ri >= ci
    strict = ri > ci
    tril = jnp.where(lower, 1.0, 0.0).astype(BF16)
    eye = jnp.where(ri == ci, 1.0, 0.0)
    levels = [((ri >> (s + 1)) == (ci >> (s + 1))) & ((ri >> s) != (ci >> s))
              for s in range(csz.bit_length() - 1)]

    probs = [(b, h) for b in range(bb) for h in range(N_HEADS)]
    each = lambda fn, *lists: [fn(*args) for args in zip(*lists)]

    rws, cums = [], []
    for b in range(bb):
        rw = rw_ref[b]
        if c_in < csz:
            rw = jnp.concatenate([rw, jnp.zeros((csz - c_in, RW_W), F32)], axis=0)
        lh, ll = _split(rw[:, D_BR:2 * D_BR])
        cums.append(jnp.dot(tril, lh, preferred_element_type=F32)
                    + jnp.dot(tril, ll, preferred_element_type=F32))
        rws.append(rw)

    def operands(b, h):
        lo = h * HEAD_DIM
        r, logw, k, v, kk, bv, g = (rws[b][:, i * D_BR + lo:i * D_BR + lo + HEAD_DIM] for i in range(7))
        cum = cums[b][:, lo:lo + HEAD_DIM]
        tot = cum[csz - 1:csz, :]
        e_inv = jnp.exp(-cum)
        e_rest = jnp.exp(tot - cum)
        return dict(r=r, k=k, v=v, g=g, kt=kk * jnp.exp(cum - logw), bt=bv * e_inv, kd=k * e_inv,
                    rt=r * jnp.exp(cum), bh=bv * e_rest, kh=k * e_rest, gam=jnp.exp(tot))

    ops = [operands(b, h) for b, h in probs]
    get = lambda name: [o[name] for o in ops]
    kt, bt, kd, rt, v = get("kt"), get("bt"), get("kd"), get("rt"), get("v")

    kr = each(lambda a, c: jnp.concatenate([a, c], axis=0), kt, rt)
    pb = each(_mm_nt, kr, bt)
    pk = each(_mm_nt, kr, kd)
    a_b = [jnp.where(strict, p[0:csz], 0.0) for p in pb]
    m_b = [jnp.where(lower, p[csz:2 * csz], 0.0) for p in pb]
    a_k = [jnp.where(strict, p[0:csz], 0.0) for p in pk]
    m_k = [jnp.where(lower, p[csz:2 * csz], 0.0) for p in pk]

    tinv = [eye - jnp.where(levels[0], a, 0.0) for a in a_b]
    for lv in levels[1:]:
        x = each(lambda a, t: _mm(jnp.where(lv, a, 0.0), t), a_b, tinv)
        tinv = each(lambda t, xx: t - _mm(t, xx), tinv, x)

    akv = each(_mm, a_k, v)
    w1 = each(_mm, tinv, kt)
    w2 = each(_mm, tinv, akv)
    qt = each(lambda r_, m, w: r_ - _mm(m, w), rt, m_b, w1)
    y0 = each(lambda mk, vv, mb, w: _mm(mk, vv) - _mm(mb, w), m_k, v, m_b, w2)
    m_neg = each(_mm_tn, w1, get("bh"))
    n_add = each(lambda vv, w, kh_, bh_: _mm_tn(jnp.concatenate([vv, w], axis=0),
                                                jnp.concatenate([kh_, -bh_], axis=0)),
                 v, w2, get("kh"), get("bh"))

    s_prev = [s_scr[b, h] for b, h in probs]
    y = each(lambda q_, s_, y_: _mm_nt(q_, s_) + y_, qt, s_prev, y0)
    s_new = each(lambda s_, o, mn, na: s_ * o["gam"] - _mm(s_, mn) + na, s_prev, ops, m_neg, n_add)
    for (b, h), s_ in zip(probs, s_new):
        s_scr[b, h] = s_

    for (b, h), o, y_ in zip(probs, ops, y):
        sl = slice(h * HEAD_DIM, (h + 1) * HEAD_DIM)
        mean = jnp.mean(y_, axis=-1, keepdims=True)
        yc = y_ - mean
        var = jnp.mean(yc * yc, axis=-1, keepdims=True)
        yn = yc * lax.rsqrt(var + LNX_EPS) * lng_ref[:, sl] + lnb_ref[:, sl]
        bonus = jnp.sum(o["r"] * o["k"] * rk_ref[:, sl], axis=-1, keepdims=True) * o["v"]
        y_ref[b, :, sl] = ((yn + bonus) * o["g"])[0:c_in].astype(BF16)

    @pl.when(c_idx == nc - 1)
    def _():
        sout_ref[...] = s_scr[...]


def _rwkv_call(rw, s0, p, l, csz, bb):
    bsz, t, _ = rw.shape
    nc = t // csz
    row = lambda: pl.BlockSpec((None, 1, D_BR), lambda i, j: (l, 0, 0))
    st = pl.BlockSpec((bb, N_HEADS, HEAD_DIM, HEAD_DIM), lambda i, j: (i, 0, 0, 0))
    return pl.pallas_call(
        functools.partial(_rwkv_kernel, nc=nc, bb=bb),
        grid=(bsz // bb, nc),
        in_specs=[pl.BlockSpec((bb, csz, RW_W), lambda i, j: (i, j, 0)), st, row(), row(), row()],
        out_specs=[pl.BlockSpec((bb, csz, D_BR), lambda i, j: (i, j, 0)), st],
        out_shape=[jax.ShapeDtypeStruct((bsz, t, D_BR), BF16),
                   jax.ShapeDtypeStruct((bsz, N_HEADS, HEAD_DIM, HEAD_DIM), F32)],
        scratch_shapes=[pltpu.VMEM((bb, N_HEADS, HEAD_DIM, HEAD_DIM), F32)],
        compiler_params=_cparams(("parallel", "arbitrary")),
        name="rwkv",
    )(rw, s0, p["r_k"], p["lnx_g"], p["lnx_b"])


def _sb_block(z, causal, tri, carry):
    ls, lk = _log_sigmoids(z)
    if causal is not None:
        lk = jnp.where(causal, lk, 0.0)
    after = jnp.dot(lk.astype(BF16), tri, preferred_element_type=F32)
    a = jnp.exp(ls + after + carry)
    if causal is not None:
        a = jnp.where(causal, a, 0.0)
    return a, carry + after[:, 0:1] + lk[:, 0:1]


def _sba_prompt_kernel(qi_ref, kb_ref, first_ref, last_ref, bias_ref, q_ref, k_ref, v_ref, tri_ref,
                       o_ref, acc, carry, *, tq):
    p = pl.program_id(1)
    first = first_ref[p] == 1
    half = tq // 2

    @pl.when(first)
    def _():
        acc[...] = jnp.zeros_like(acc)
        carry[...] = jnp.zeros_like(carry)

    def compute(diag):
        q = q_ref[0]
        kb = k_ref[0]
        vb = v_ref[0]
        tri = tri_ref[...]
        causal = None
        if diag:
            causal = (lax.broadcasted_iota(jnp.int32, (tq, tq), 1)
                      < lax.broadcasted_iota(jnp.int32, (tq, tq), 0))
        for h in range(N_HEADS):
            sl = slice(h * HEAD_DIM, (h + 1) * HEAD_DIM)
            z = lax.dot_general(q[:, sl], kb[:, sl], (((1,), (1,)), ((), ())),
                                preferred_element_type=F32) + bias_ref[h]
            ls, lk = _log_sigmoids(z)
            if diag:
                lk = jnp.where(causal, lk, 0.0)
            out = None
            c = carry[h]
            for lo in (half, 0):
                lk_h = lk[:, lo:lo + half]
                inner = jnp.dot(lk_h.astype(BF16), tri, preferred_element_type=F32)
                a = jnp.exp(ls[:, lo:lo + half] + inner)
                if diag:
                    a = jnp.where(causal[:, lo:lo + half], a, 0.0)
                part = jnp.exp(c) * jnp.dot(a.astype(BF16), vb[lo:lo + half, sl],
                                            preferred_element_type=F32)
                out = part if out is None else out + part
                c = c + inner[:, 0:1] + lk_h[:, 0:1]
            carry[h] = c
            acc[:, sl] += out

    @pl.when(first)
    def _():
        compute(True)

    @pl.when(jnp.logical_not(first))
    def _():
        compute(False)

    @pl.when(last_ref[p] == 1)
    def _():
        o_ref[0] = acc[...].astype(BF16)


def _sba_prompt_call(q, k, v, bias, tri, tq):
    bsz, t, _ = q.shape
    nq = t // tq
    pairs = [(i, i - j, int(j == 0), int(j == i)) for i in range(nq) for j in range(i + 1)]
    qi_tbl, kb_tbl, first_tbl, last_tbl = (jnp.asarray(c, jnp.int32) for c in zip(*pairs))
    q_map = lambda b, p, qi, kb, fi, la: (b, qi[p], 0)
    kv_map = lambda b, p, qi, kb, fi, la: (b, kb[p], 0)
    grid_spec = pltpu.PrefetchScalarGridSpec(
        num_scalar_prefetch=4,
        grid=(bsz, len(pairs)),
        in_specs=[pl.BlockSpec(memory_space=pltpu.SMEM),
                  pl.BlockSpec((1, tq, D_BR), q_map),
                  pl.BlockSpec((1, tq, D_BR), kv_map),
                  pl.BlockSpec((1, tq, D_BR), kv_map),
                  pl.BlockSpec((tq // 2, tq // 2), lambda b, p, qi, kb, fi, la: (0, 0))],
        out_specs=pl.BlockSpec((1, tq, D_BR), q_map),
        scratch_shapes=[pltpu.VMEM((tq, D_BR), F32), pltpu.VMEM((N_HEADS, tq, 1), F32)],
    )
    return pl.pallas_call(
        functools.partial(_sba_prompt_kernel, tq=tq),
        grid_spec=grid_spec,
        out_shape=jax.ShapeDtypeStruct((bsz, t, D_BR), BF16),
        compiler_params=_cparams(("parallel", "arbitrary")),
        name="sba_prompt",
    )(qi_tbl, kb_tbl, first_tbl, last_tbl, bias, q, k, v, tri)


def _sba_sample_kernel(pt_ref, bias_ref, q_ref, kn_ref, vn_ref, tri_ref, *rest, npg, tpad, nsteps):
    k_refs = rest[:npg]
    v_refs = rest[npg:2 * npg]
    o_ref = rest[2 * npg]
    acc, carry = rest[2 * npg + 1:]
    s = pl.program_id(1)
    rows = N_HEADS * tpad
    tri = tri_ref[...]
    tshift = tpad.bit_length() - 1
    rid = lax.broadcasted_iota(jnp.int32, (rows, D_BR), 0) >> tshift
    cid = lax.broadcasted_iota(jnp.int32, (rows, D_BR), 1) >> 6
    head_match = rid == cid
    q = q_ref[0].astype(F32)
    q_bd = jnp.where(head_match, jnp.concatenate([q] * N_HEADS, axis=0), 0.0).astype(BF16)
    hrow = lax.broadcasted_iota(jnp.int32, (rows, 1), 0) >> tshift
    bias = jnp.zeros((rows, 1), F32)
    for h in range(N_HEADS):
        bias = jnp.where(hrow == h, bias_ref[h], bias)

    @pl.when(s == 0)
    def _():
        kpos = lax.broadcasted_iota(jnp.int32, (rows, PAGE), 1)
        qpos = lax.broadcasted_iota(jnp.int32, (rows, PAGE), 0) & (tpad - 1)
        z = lax.dot_general(q_bd, kn_ref[0].astype(BF16), (((1,), (1,)), ((), ())),
                            preferred_element_type=F32) + bias
        a, c_new = _sb_block(z, kpos < qpos, tri, jnp.zeros((rows, 1), F32))
        carry[...] = c_new
        acc[...] = jnp.dot(a.astype(BF16), vn_ref[0].astype(BF16), preferred_element_type=F32)

    kt = jnp.concatenate([r[0, 0] for r in k_refs], axis=1).astype(BF16)
    vt = jnp.concatenate([r[0, 0] for r in v_refs], axis=1).astype(BF16)
    z = jnp.dot(q_bd, kt, preferred_element_type=F32) + bias
    ls, lk = _log_sigmoids(z)
    page = lambda x, i: x[:, i * PAGE:(i + 1) * PAGE]
    local = [jnp.dot(page(lk, i).astype(BF16), tri, preferred_element_type=F32) for i in range(npg)]
    totals = [local[i][:, 0:1] + page(lk, i)[:, 0:1] for i in range(npg)]
    c = carry[...]
    after = [None] * npg
    for i in range(npg - 1, -1, -1):
        after[i] = local[i] + c
        c = c + totals[i]
    carry[...] = c
    a = jnp.exp(ls + jnp.concatenate(after, axis=1))
    acc[...] += lax.dot_general(a.astype(BF16), vt, (((1,), (1,)), ((), ())), preferred_element_type=F32)

    @pl.when(s == nsteps - 1)
    def _():
        masked = jnp.where(head_match, acc[...], 0.0)
        out = masked[0:tpad]
        for h in range(1, N_HEADS):
            out = out + masked[h * tpad:(h + 1) * tpad]
        o_ref[0] = out.astype(BF16)


def _sba_sample_call(q, kn, vn, cache_k, cache_v, page_table, bias, tri, l, npg):
    bsz, tpad, _ = q.shape
    n_pages = page_table.shape[1]
    nsteps = n_pages // npg
    rows = N_HEADS * tpad

    def page_spec(i):
        return pl.BlockSpec((1, 1, D_BR, PAGE),
                            lambda b, s, pt: (l, pt[b, n_pages - (s + 1) * npg + i], 0, 0))

    grid_spec = pltpu.PrefetchScalarGridSpec(
        num_scalar_prefetch=1,
        grid=(bsz, nsteps),
        in_specs=[pl.BlockSpec(memory_space=pltpu.SMEM),
                  pl.BlockSpec((1, tpad, D_BR), lambda b, s, pt: (b, 0, 0)),
                  pl.BlockSpec((1, PAGE, D_BR), lambda b, s, pt: (b, 0, 0)),
                  pl.BlockSpec((1, PAGE, D_BR), lambda b, s, pt: (b, 0, 0)),
                  pl.BlockSpec((PAGE, PAGE), lambda b, s, pt: (0, 0))]
        + [page_spec(i) for i in range(npg)] + [page_spec(i) for i in range(npg)],
        out_specs=pl.BlockSpec((1, tpad, D_BR), lambda b, s, pt: (b, 0, 0)),
        scratch_shapes=[pltpu.VMEM((rows, D_BR), F32), pltpu.VMEM((rows, 1), F32)],
    )
    return pl.pallas_call(
        functools.partial(_sba_sample_kernel, npg=npg, tpad=tpad, nsteps=nsteps),
        grid_spec=grid_spec,
        out_shape=jax.ShapeDtypeStruct((bsz, tpad, D_BR), BF16),
        compiler_params=_cparams(("parallel", "arbitrary")),
        name="sba_sample",
    )(page_table, bias, q, kn, vn, tri, *([cache_k] * npg), *([cache_v] * npg))


def _merge_kernel(x_ref, mod_ref, u_ref, ya_ref, yb_ref, yc_ref, yd_ref, wg_ref, wb_ref, wo_ref, o_ref):
    x = x_ref[...]
    bb, tm, _ = x.shape
    rows = bb * tm
    ub = u_ref[...].reshape(rows, D_MODEL)
    merged = jnp.zeros((rows, D_MODEL), F32)
    for i, y_ref in enumerate((ya_ref, yb_ref, yc_ref, yd_ref)):
        gate = _sigmoid(jnp.dot(ub, wg_ref[i], preferred_element_type=F32))
        br = jnp.dot(y_ref[...].reshape(rows, D_BR), wb_ref[i], preferred_element_type=F32)
        merged = merged + gate * br
    out = jnp.dot(merged.astype(BF16), wo_ref[...], preferred_element_type=F32)
    g2 = mod_ref[:, 5:6, :]
    o_ref[...] = x + g2 * out.reshape(bb, tm, D_MODEL)


def _merge_call(x, mod, ub, ya, yb, yc, yd, wg, wb, wo, l, bb, tm):
    bsz, t, _ = x.shape
    tok = lambda w: pl.BlockSpec((bb, tm, w), lambda i, j: (i, j, 0))
    return pl.pallas_call(
        _merge_kernel,
        grid=(bsz // bb, t // tm),
        in_specs=[tok(D_MODEL), pl.BlockSpec((bb, N_MOD, D_MODEL), lambda i, j: (i, 0, 0)),
                  tok(D_MODEL), tok(D_BR), tok(D_BR), tok(D_BR), tok(D_BR),
                  _resident((None, N_BRANCH, D_MODEL, D_MODEL), lambda i, j: (l, 0, 0, 0)),
                  _resident((None, N_BRANCH, D_BR, D_MODEL), lambda i, j: (l, 0, 0, 0)),
                  _resident((None, D_MODEL, D_MODEL), lambda i, j: (l, 0, 0))],
        out_specs=tok(D_MODEL),
        out_shape=jax.ShapeDtypeStruct(x.shape, F32),
        compiler_params=_cparams(("parallel", "parallel")),
        name="merge",
    )(x, mod, ub, ya, yb, yc, yd, wg, wb, wo)


def _tri_matrix(n):
    r = lax.broadcasted_iota(jnp.int32, (n, n), 0)
    c = lax.broadcasted_iota(jnp.int32, (n, n), 1)
    return jnp.where(r > c, 1.0, 0.0).astype(BF16)


def kernel(x_prompt, x_sample, c_prompt, c_sample, cache_k, cache_v, page_table, state_conv, state_shift, state_wkv, state_pool, ada_w, ada_b, norm_g, w_ffn_up, w_ffn_down, w_in, conv_w, rwkv_mu, rwkv_w0, rwkv_w2, rwkv_a0, rwkv_a2, rwkv_g2, rwkv_k_k, rwkv_k_a, rwkv_r_k, rwkv_lnx_g, rwkv_lnx_b, pool_w, pool_scale, q_norm_g, k_norm_g, sb_bias, w_gate, w_branch, w_out):
    depth = ada_w.shape[0]
    bp, seq, _ = x_prompt.shape
    bs, dec_seq, _ = x_sample.shape
    n_phys = cache_k.shape[1]
    past_len = page_table.shape[1] * PAGE
    tpad = 8
    tm_p = 512
    tq_p = 512

    wup = w_ffn_up.astype(BF16)
    wdn = w_ffn_down.astype(BF16)
    win = w_in.astype(BF16)
    wg = w_gate.astype(BF16)
    wb = w_branch.astype(BF16)
    wo = w_out.astype(BF16)
    ng = norm_g.reshape(depth * 3, 1, D_MODEL)
    pool_bd = jnp.zeros((depth, len(POOL_WINDOWS), 64, len(POOL_WINDOWS), 64), F32)
    for gi in range(len(POOL_WINDOWS)):
        pool_bd = pool_bd.at[:, gi, :, gi, :].set(pool_w[:, gi])
    prm = {
        "conv_w": conv_w, "mu": rwkv_mu[:, None, :], "w0": rwkv_w0[:, None, :], "w2": rwkv_w2,
        "a0": rwkv_a0[:, None, :], "a2": rwkv_a2, "g2": rwkv_g2, "k_k": rwkv_k_k[:, None, :],
        "k_a": rwkv_k_a[:, None, :], "pool_bd": pool_bd.reshape(depth, D_BR, D_BR).astype(BF16),
        "pool_scale": pool_scale[:, None, :],
        "q_g": jnp.tile(q_norm_g, (1, N_HEADS))[:, None, :],
        "k_g": jnp.tile(k_norm_g, (1, N_HEADS))[:, None, :],
        "r_k": rwkv_r_k[:, None, :], "lnx_g": rwkv_lnx_g[:, None, :], "lnx_b": rwkv_lnx_b[:, None, :],
    }
    ck = jnp.transpose(cache_k, (0, 1, 3, 4, 2)).reshape(depth, n_phys, D_BR, PAGE)
    cv = jnp.transpose(cache_v, (0, 1, 3, 4, 2)).reshape(depth, n_phys, D_BR, PAGE)
    tri_p = _tri_matrix(tq_p // 2)
    tri_s = _tri_matrix(PAGE)

    n_c = bp + bs
    c_all = jnp.concatenate([c_prompt, c_sample, jnp.zeros((-n_c % 8, D_MODEL), F32)], axis=0)
    mod_all = _ada_call(c_all, ada_w, ada_b)
    xs_pad = jnp.concatenate([x_sample, jnp.zeros((bs, tpad - dec_seq, D_MODEL), F32)], axis=1)

    halo_p = jnp.zeros((bp, HALO, EXT_W), F32)
    wkv0_p = jnp.zeros((bp, N_HEADS, HEAD_DIM, HEAD_DIM), F32)

    hp, hs = x_prompt, xs_pad
    outs_p = [[] for _ in range(6)]
    outs_s = [[] for _ in range(6)]
    for l in range(depth):
        mod_p = mod_all[l, :bp].reshape(bp, N_MOD, D_MODEL)
        mod_s = mod_all[l, bp:n_c].reshape(bs, N_MOD, D_MODEL)
        halo_s = jnp.zeros((bs, HALO, EXT_W), F32)
        halo_s = halo_s.at[:, HALO - (CONV_W - 1):, 0:D_BR].set(state_conv[l])
        halo_s = halo_s.at[:, HALO - 1, D_BR:D_BR + D_MODEL].set(state_shift[l])
        halo_s = halo_s.at[:, HALO - POOL_CTX:, D_BR + D_MODEL:].set(state_pool[l])

        for grp in ("p", "s"):
            if grp == "p":
                x, mod, bb, tm, t_real, pos0, halo0, wkv0 = hp, mod_p, 1, tm_p, seq, 0, halo_p, wkv0_p
            else:
                x, mod, bb, tm, t_real, pos0, halo0, wkv0 = hs, mod_s, bs, tpad, dec_seq, past_len, halo_s, state_wkv[l]
            x = _ffn_call(x, mod, ng, wup, wdn, l, 0, bb, tm)
            ub, ya, yc, rw, q, k, v, k16, v16, conv_n, shift_n, pool_n = _mix_call(
                x, mod, ng, win, halo0, prm, l, bb, tm, t_real, pos0)
            seqs = max(d for d in (1, 2, 4) if rw.shape[0] % d == 0)
            yb, wkv_n = _rwkv_call(rw, wkv0, prm, l, RWKV_CHUNK if grp == "p" else tpad, seqs)
            if grp == "p":
                yd = _sba_prompt_call(q, k16, v16, sb_bias[l], tri_p, tq_p)
            else:
                kn = jnp.concatenate([k, jnp.zeros((bs, PAGE - tpad, D_BR), F32)], axis=1)
                vn = jnp.concatenate([v, jnp.zeros((bs, PAGE - tpad, D_BR), F32)], axis=1)
                yd = _sba_sample_call(q, kn, vn, ck, cv, page_table, sb_bias[l], tri_s, l,
                                      16 if page_table.shape[1] % 16 == 0 else 8)
            x = _merge_call(x, mod, ub, ya, yb, yc, yd, wg, wb, wo, l, bb, tm)
            x = _ffn_call(x, mod, ng, wup, wdn, l, 1, bb, tm)
            new = (k[:, :t_real].reshape(-1, t_real, N_HEADS, HEAD_DIM),
                   v[:, :t_real].reshape(-1, t_real, N_HEADS, HEAD_DIM),
                   conv_n, shift_n[:, 0], wkv_n, pool_n)
            if grp == "p":
                hp = x
                for lst, a in zip(outs_p, new):
                    lst.append(a)
            else:
                hs = x
                for lst, a in zip(outs_s, new):
                    lst.append(a)

    stacked_p = [jnp.stack(a, axis=0) for a in outs_p]
    stacked_s = [jnp.stack(a, axis=0) for a in outs_s]
    return (hp, hs[:, :dec_seq], *stacked_p, *stacked_s)
```

```python
import functools

import jax
import jax.numpy as jnp
from jax import lax
from jax.experimental import pallas as pl
from jax.experimental.pallas import tpu as pltpu

F32 = jnp.float32
BF16 = jnp.bfloat16

D_MODEL = 1024
N_BRANCH = 4
D_BR = 256
HEAD_DIM = 64
N_HEADS = 4
CONV_W = 3
POOL_WINDOWS = (2, 4, 8, 16)
POOL_CTX = 15
D_FF = 2816
D_IN = 2816
N_MOD = 9
PAGE = 128
EPS = 1e-6
LNX_EPS = 64e-5

HALO = 16
EXT_W = D_BR + D_MODEL + D_BR
RW_W = 7 * D_BR
RWKV_CHUNK = 64
VMEM_LIMIT = 56 * 1024 * 1024


def _cparams(sem):
    return pltpu.CompilerParams(dimension_semantics=sem, vmem_limit_bytes=VMEM_LIMIT)


def _resident(shape, index_map):
    return pl.BlockSpec(shape, index_map, pipeline_mode=pl.Buffered(1))


def _mm(a, b):
    return jnp.dot(a.astype(BF16), b.astype(BF16), preferred_element_type=F32)


def _mm_nt(a, b):
    return lax.dot_general(a.astype(BF16), b.astype(BF16), (((1,), (1,)), ((), ())),
                           preferred_element_type=F32)


def _mm_tn(a, b):
    return lax.dot_general(a.astype(BF16), b.astype(BF16), (((0,), (0,)), ((), ())),
                           preferred_element_type=F32)


def _split(a):
    hi = a.astype(BF16)
    lo = (a - hi.astype(F32)).astype(BF16)
    return hi, lo


def _mm3(a, b):
    ah, al = _split(a)
    bh, bl = _split(b)
    out = jnp.dot(ah, bh, preferred_element_type=F32)
    out = out + jnp.dot(ah, bl, preferred_element_type=F32)
    return out + jnp.dot(al, bh, preferred_element_type=F32)


def _softplus(y):
    return jnp.maximum(y, 0.0) + jnp.log(1.0 + jnp.exp(-jnp.abs(y)))


def _sigmoid(y):
    return 1.0 / (1.0 + jnp.exp(-y))


def _neg_abs(y):
    bits = lax.bitcast_convert_type(y, jnp.uint32) | jnp.uint32(0x80000000)
    return lax.bitcast_convert_type(bits, F32)


def _log_sigmoids(z):
    ls = jnp.minimum(z, 0.0) - jnp.log(1.0 + jnp.exp(_neg_abs(z)))
    return ls, ls - z


def _head_sum_matrix():
    r = lax.broadcasted_iota(jnp.int32, (D_BR, D_BR), 0) >> 6
    c = lax.broadcasted_iota(jnp.int32, (D_BR, D_BR), 1) >> 6
    return jnp.where(r == c, 1.0, 0.0).astype(BF16)


def _modulated_norm(x, mod_ref, ng_ref, mi):
    sh = mod_ref[:, 3 * mi:3 * mi + 1, :]
    sc = mod_ref[:, 3 * mi + 1:3 * mi + 2, :]
    ms = jnp.mean(x * x, axis=-1, keepdims=True)
    u = x * lax.rsqrt(ms + EPS) * ng_ref[...]
    return u * (1.0 + sc) + sh


def _ada_kernel(c_ref, w_ref, b_ref, o_ref):
    c = c_ref[...]
    s = c * _sigmoid(c)
    o_ref[0] = _mm3(s, w_ref[0]) + b_ref[0]


def _ada_call(c_all, ada_w, ada_b):
    depth = ada_w.shape[0]
    rows = c_all.shape[0]
    n_out = ada_w.shape[2]
    tn = 1152
    return pl.pallas_call(
        _ada_kernel,
        grid=(depth, n_out // tn),
        in_specs=[pl.BlockSpec((rows, D_MODEL), lambda l, j: (0, 0)),
                  pl.BlockSpec((1, D_MODEL, tn), lambda l, j: (l, 0, j)),
                  pl.BlockSpec((1, 1, tn), lambda l, j: (l, 0, j))],
        out_specs=pl.BlockSpec((1, rows, tn), lambda l, j: (l, 0, j)),
        out_shape=jax.ShapeDtypeStruct((depth, rows, n_out), F32),
        compiler_params=_cparams(("parallel", "parallel")),
        name="ada_mod",
    )(c_all, ada_w, ada_b.reshape(depth, 1, n_out))


def _ffn_kernel(x_ref, mod_ref, ng_ref, wup_ref, wdn_ref, o_ref, *, mi, nf):
    x = x_ref[...]
    bb, tm, _ = x.shape
    u = _modulated_norm(x, mod_ref, ng_ref, mi)
    ub = u.reshape(bb * tm, D_MODEL).astype(BF16)
    tf = D_FF // nf
    acc = jnp.zeros((bb * tm, D_MODEL), F32)
    for j in range(nf):
        a = jnp.dot(ub, wup_ref[:, j * tf:(j + 1) * tf], preferred_element_type=F32)
        b = jnp.dot(ub, wup_ref[:, D_FF + j * tf:D_FF + (j + 1) * tf], preferred_element_type=F32)
        h = (a * _sigmoid(a) * b).astype(BF16)
        acc = acc + jnp.dot(h, wdn_ref[j * tf:(j + 1) * tf, :], preferred_element_type=F32)
    g = mod_ref[:, 3 * mi + 2:3 * mi + 3, :]
    o_ref[...] = x + 0.5 * g * acc.reshape(bb, tm, D_MODEL)


def _ffn_call(x, mod, ng, wup, wdn, l, which, bb, tm, nf=2):
    bsz, t, _ = x.shape
    mi = 0 if which == 0 else 2
    kern = functools.partial(_ffn_kernel, mi=mi, nf=nf)
    return pl.pallas_call(
        kern,
        grid=(bsz // bb, t // tm),
        in_specs=[pl.BlockSpec((bb, tm, D_MODEL), lambda i, j: (i, j, 0)),
                  pl.BlockSpec((bb, N_MOD, D_MODEL), lambda i, j: (i, 0, 0)),
                  pl.BlockSpec((None, 1, D_MODEL), lambda i, j: (3 * l + mi, 0, 0)),
                  _resident((None, None, D_MODEL, 2 * D_FF), lambda i, j: (l, which, 0, 0)),
                  _resident((None, None, D_FF, D_MODEL), lambda i, j: (l, which, 0, 0))],
        out_specs=pl.BlockSpec((bb, tm, D_MODEL), lambda i, j: (i, j, 0)),
        out_shape=jax.ShapeDtypeStruct(x.shape, F32),
        compiler_params=_cparams(("parallel", "parallel")),
        name="ffn",
    )(x, mod, ng, wup, wdn)


def _mix_kernel(x_ref, mod_ref, ng_ref, win_ref, halo_ref, convw_ref, mu_ref, w0_ref, ww2_ref,
                a0_ref, wa2_ref, wg2_ref, kk_ref, ka_ref, poolw_ref, pools_ref, qg_ref, kg_ref,
                u_ref, ya_ref, yc_ref, rw_ref, q_ref, k_ref, v_ref, kb_ref, vb_ref,
                conv_ref, shift_ref, pool_ref, ext, *, tm, t_real, pos0, nt):
    t = pl.program_id(1)

    @pl.when(t == 0)
    def _():
        ext[:, 0:HALO, :] = halo_ref[...]

    x = x_ref[...]
    bb = x.shape[0]
    rows = bb * tm
    u = _modulated_norm(x, mod_ref, ng_ref, 1)
    ub = u.reshape(rows, D_MODEL).astype(BF16)
    u_ref[...] = ub.reshape(bb, tm, D_MODEL)
    proj = jnp.dot(ub, win_ref[...], preferred_element_type=F32)

    bg = proj[:, 0:D_BR]
    z = proj[:, D_BR:2 * D_BR] * proj[:, 2 * D_BR:3 * D_BR]
    p_rwkv = proj[:, 3 * D_BR:3 * D_BR + D_MODEL]
    p_pool = proj[:, 3 * D_BR + D_MODEL:4 * D_BR + D_MODEL]
    ext[:, HALO:HALO + tm, 0:D_BR] = z.reshape(bb, tm, D_BR)
    ext[:, HALO:HALO + tm, D_BR:D_BR + D_MODEL] = p_rwkv.reshape(bb, tm, D_MODEL)
    ext[:, HALO:HALO + tm, D_BR + D_MODEL:EXT_W] = p_pool.reshape(bb, tm, D_BR)
    cw = convw_ref[...]
    z1 = ext[:, HALO - 1:HALO - 1 + tm, 0:D_BR].reshape(rows, D_BR)
    z2 = ext[:, HALO - 2:HALO - 2 + tm, 0:D_BR].reshape(rows, D_BR)
    ya = bg * (cw[0:1] * z2 + cw[1:2] * z1 + cw[2:3] * z)
    ya_ref[...] = ya.astype(BF16).reshape(bb, tm, D_BR)

    pp = ext[:, HALO - 1:HALO - 1 + tm, D_BR:D_BR + D_MODEL].reshape(rows, D_MODEL)
    xs = p_rwkv + (pp - p_rwkv) * mu_ref[...]
    r = xs[:, 0:D_BR]
    k = xs[:, D_BR:2 * D_BR]
    v = xs[:, 2 * D_BR:3 * D_BR]
    wd = xs[:, 3 * D_BR:3 * D_BR + 64]
    ad = xs[:, 3 * D_BR + 64:3 * D_BR + 128]
    gd = xs[:, 3 * D_BR + 128:D_MODEL]
    w_log = -_softplus(-(w0_ref[...] + _mm(jnp.tanh(wd), ww2_ref[...]))) - 0.5
    logw = -jnp.exp(w_log)
    a = _sigmoid(a0_ref[...] + _mm(ad, wa2_ref[...]))
    g = _mm(_sigmoid(gd), wg2_ref[...])
    hsum = _head_sum_matrix()
    kkr = k * kk_ref[...]
    kk = kkr / jnp.maximum(jnp.sqrt(_mm(kkr * kkr, hsum)), 1e-12)
    kmod = k * (1.0 + (a - 1.0) * ka_ref[...])
    bvec = kk * a
    tok = t * tm + (lax.broadcasted_iota(jnp.int32, (rows, 1), 0) & (tm - 1))
    if t_real < nt * tm:
        live = jnp.where(tok < t_real, 1.0, 0.0)
        logw, kmod, v, kk, bvec = logw * live, kmod * live, v * live, kk * live, bvec * live
    for i, val in enumerate((r, logw, kmod, v, kk, bvec, g)):
        rw_ref[:, :, i * D_BR:(i + 1) * D_BR] = val.reshape(bb, tm, D_BR)

    col = lax.broadcasted_iota(jnp.int32, (1, D_BR), 1)
    acc = p_pool
    sel = None
    for i in range(1, POOL_WINDOWS[-1]):
        acc = acc + ext[:, HALO - i:HALO - i + tm, D_BR + D_MODEL:EXT_W].reshape(rows, D_BR)
        if (i + 1) in POOL_WINDOWS:
            gi = POOL_WINDOWS.index(i + 1)
            sel = acc if sel is None else jnp.where(col >= gi * 64, acc, sel)
    wcol = jnp.where(col < 64, 2, jnp.where(col < 128, 4, jnp.where(col < 192, 8, 16)))
    cnt = jnp.minimum(wcol, pos0 + tok + 1).astype(F32)
    dd = sel / cnt - p_pool
    yc = _mm(dd, poolw_ref[...]) * pools_ref[...]
    yc_ref[...] = yc.astype(BF16).reshape(bb, tm, D_BR)

    q = proj[:, 4 * D_BR + D_MODEL:5 * D_BR + D_MODEL]
    ks = proj[:, 5 * D_BR + D_MODEL:6 * D_BR + D_MODEL]
    vs = proj[:, 6 * D_BR + D_MODEL:D_IN]
    qn = q * lax.rsqrt(_mm(q * q, hsum) * (1.0 / HEAD_DIM) + EPS) * qg_ref[...]
    kn = ks * lax.rsqrt(_mm(ks * ks, hsum) * (1.0 / HEAD_DIM) + EPS) * kg_ref[...]
    q_ref[...] = (qn * HEAD_DIM ** -0.5).astype(BF16).reshape(bb, tm, D_BR)
    k_ref[...] = kn.reshape(bb, tm, D_BR)
    v_ref[...] = vs.reshape(bb, tm, D_BR)
    kb_ref[...] = kn.astype(BF16).reshape(bb, tm, D_BR)
    vb_ref[...] = vs.astype(BF16).reshape(bb, tm, D_BR)

    @pl.when(t == nt - 1)
    def _():
        last = HALO + t_real - (nt - 1) * tm - 1
        conv_ref[...] = ext[:, last - (CONV_W - 2):last + 1, 0:D_BR]
        shift_ref[...] = ext[:, last:last + 1, D_BR:D_BR + D_MODEL]
        pool_ref[...] = ext[:, last - (POOL_CTX - 1):last + 1, D_BR + D_MODEL:EXT_W]

    if nt > 1:
        ext[:, 0:HALO, :] = ext[:, tm:tm + HALO, :]


def _mix_call(x, mod, ng, win, halo0, p, l, bb, tm, t_real, pos0):
    bsz, t, _ = x.shape
    nt = t // tm
    kern = functools.partial(_mix_kernel, tm=tm, t_real=t_real, pos0=pos0, nt=nt)
    tok = lambda w: pl.BlockSpec((bb, tm, w), lambda i, j: (i, j, 0))
    per_b = lambda r, w: pl.BlockSpec((bb, r, w), lambda i, j: (i, 0, 0))
    row = lambda r, w: pl.BlockSpec((None, r, w), lambda i, j: (l, 0, 0))
    sds = lambda shape, dt: jax.ShapeDtypeStruct(shape, dt)
    return pl.pallas_call(
        kern,
        grid=(bsz // bb, nt),
        in_specs=[tok(D_MODEL), per_b(N_MOD, D_MODEL),
                  pl.BlockSpec((None, 1, D_MODEL), lambda i, j: (3 * l + 1, 0, 0)),
                  _resident((None, D_MODEL, D_IN), lambda i, j: (l, 0, 0)),
                  per_b(HALO, EXT_W),
                  row(CONV_W, D_BR), row(1, D_MODEL), row(1, D_BR), row(64, D_BR), row(1, D_BR),
                  row(64, D_BR), row(128, D_BR), row(1, D_BR), row(1, D_BR), row(D_BR, D_BR),
                  row(1, D_BR), row(1, D_BR), row(1, D_BR)],
        out_specs=[tok(D_MODEL), tok(D_BR), tok(D_BR), tok(RW_W), tok(D_BR), tok(D_BR), tok(D_BR),
                   tok(D_BR), tok(D_BR),
                   per_b(CONV_W - 1, D_BR), per_b(1, D_MODEL), per_b(POOL_CTX, D_BR)],
        out_shape=[sds((bsz, t, D_MODEL), BF16), sds((bsz, t, D_BR), BF16), sds((bsz, t, D_BR), BF16),
                   sds((bsz, t, RW_W), F32), sds((bsz, t, D_BR), BF16), sds((bsz, t, D_BR), F32),
                   sds((bsz, t, D_BR), F32), sds((bsz, t, D_BR), BF16), sds((bsz, t, D_BR), BF16),
                   sds((bsz, CONV_W - 1, D_BR), F32),
                   sds((bsz, 1, D_MODEL), F32), sds((bsz, POOL_CTX, D_BR), F32)],
        scratch_shapes=[pltpu.VMEM((bb, HALO + tm, EXT_W), F32)],
        compiler_params=_cparams(("parallel", "arbitrary")),
        name="mix_prep",
    )(x, mod, ng, win, halo0, p["conv_w"], p["mu"], p["w0"], p["w2"], p["a0"], p["a2"], p["g2"],
      p["k_k"], p["k_a"], p["pool_bd"], p["pool_scale"], p["q_g"], p["k_g"])


def _rwkv_kernel(rw_ref, s0_ref, rk_ref, lng_ref, lnb_ref, y_ref, sout_ref, s_scr, *, nc, bb):
    c_idx = pl.program_id(1)

    @pl.when(c_idx == 0)
    def _():
        s_scr[...] = s0_ref[...]

    c_in = rw_ref.shape[1]
    csz = RWKV_CHUNK
    ri = lax.broadcasted_iota(jnp.int32, (csz, csz), 0)
    ci = lax.broadcasted_iota(jnp.int32, (csz, csz), 1)
    lower = ri >= ci
    strict = ri > ci
    tril = jnp.where(lower, 1.0, 0.0).astype(BF16)
    eye = jnp.where(ri == ci, 1.0, 0.0)
    levels = [((ri >> (s + 1)) == (ci >> (s + 1))) & ((ri >> s) != (ci >> s))
              for s in range(csz.bit_length() - 1)]

    probs = [(b, h) for b in range(bb) for h in range(N_HEADS)]
    each = lambda fn, *lists: [fn(*args) for args in zip(*lists)]

    rws, cums = [], []
    for b in range(bb):
        rw = rw_ref[b]
        if c_in < csz:
            rw = jnp.concatenate([rw, jnp.zeros((csz - c_in, RW_W), F32)], axis=0)
        lh, ll = _split(rw[:, D_BR:2 * D_BR])
        cums.append(jnp.dot(tril, lh, preferred_element_type=F32)
                    + jnp.dot(tril, ll, preferred_element_type=F32))
        rws.append(rw)

    def operands(b, h):
        lo = h * HEAD_DIM
        r, logw, k, v, kk, bv, g = (rws[b][:, i * D_BR + lo:i * D_BR + lo + HEAD_DIM] for i in range(7))
        cum = cums[b][:, lo:lo + HEAD_DIM]
        tot = cum[csz - 1:csz, :]
        e_inv = jnp.exp(-cum)
        e_rest = jnp.exp(tot - cum)
        return dict(r=r, k=k, v=v, g=g, kt=kk * jnp.exp(cum - logw), bt=bv * e_inv, kd=k * e_inv,
                    rt=r * jnp.exp(cum), bh=bv * e_rest, kh=k * e_rest, gam=jnp.exp(tot))

    ops = [operands(b, h) for b, h in probs]
    get = lambda name: [o[name] for o in ops]
    kt, bt, kd, rt, v = get("kt"), get("bt"), get("kd"), get("rt"), get("v")

    kr = each(lambda a, c: jnp.concatenate([a, c], axis=0), kt, rt)
    pb = each(_mm_nt, kr, bt)
    pk = each(_mm_nt, kr, kd)
    a_b = [jnp.where(strict, p[0:csz], 0.0) for p in pb]
    m_b = [jnp.where(lower, p[csz:2 * csz], 0.0) for p in pb]
    a_k = [jnp.where(strict, p[0:csz], 0.0) for p in pk]
    m_k = [jnp.where(lower, p[csz:2 * csz], 0.0) for p in pk]

    tinv = [eye - jnp.where(levels[0], a, 0.0) for a in a_b]
    for lv in levels[1:]:
        x = each(lambda a, t: _mm(jnp.where(lv, a, 0.0), t), a_b, tinv)
        tinv = each(lambda t, xx: t - _mm(t, xx), tinv, x)

    akv = each(_mm, a_k, v)
    w1 = each(_mm, tinv, kt)
    w2 = each(_mm, tinv, akv)
    qt = each(lambda r_, m, w: r_ - _mm(m, w), rt, m_b, w1)
    y0 = each(lambda mk, vv, mb, w: _mm(mk, vv) - _mm(mb, w), m_k, v, m_b, w2)
    m_neg = each(_mm_tn, w1, get("bh"))
    n_add = each(lambda vv, w, kh_, bh_: _mm_tn(jnp.concatenate([vv, w], axis=0),
                                                jnp.concatenate([kh_, -bh_], axis=0)),
                 v, w2, get("kh"), get("bh"))

    s_prev = [s_scr[b, h] for b, h in probs]
    y = each(lambda q_, s_, y_: _mm_nt(q_, s_) + y_, qt, s_prev, y0)
    s_new = each(lambda s_, o, mn, na: s_ * o["gam"] - _mm(s_, mn) + na, s_prev, ops, m_neg, n_add)
    for (b, h), s_ in zip(probs, s_new):
        s_scr[b, h] = s_

    for (b, h), o, y_ in zip(probs, ops, y):
        sl = slice(h * HEAD_DIM, (h + 1) * HEAD_DIM)
        mean = jnp.mean(y_, axis=-1, keepdims=True)
        yc = y_ - mean
        var = jnp.mean(yc * yc, axis=-1, keepdims=True)
        yn = yc * lax.rsqrt(var + LNX_EPS) * lng_ref[:, sl] + lnb_ref[:, sl]
        bonus = jnp.sum(o["r"] * o["k"] * rk_ref[:, sl], axis=-1, keepdims=True) * o["v"]
        y_ref[b, :, sl] = ((yn + bonus) * o["g"])[0:c_in].astype(BF16)

    @pl.when(c_idx == nc - 1)
    def _():
        sout_ref[...] = s_scr[...]


def _rwkv_call(rw, s0, p, l, csz, bb):
    bsz, t, _ = rw.shape
    nc = t // csz
    row = lambda: pl.BlockSpec((None, 1, D_BR), lambda i, j: (l, 0, 0))
    st = pl.BlockSpec((bb, N_HEADS, HEAD_DIM, HEAD_DIM), lambda i, j: (i, 0, 0, 0))
    return pl.pallas_call(
        functools.partial(_rwkv_kernel, nc=nc, bb=bb),
        grid=(bsz // bb, nc),
        in_specs=[pl.BlockSpec((bb, csz, RW_W), lambda i, j: (i, j, 0)), st, row(), row(), row()],
        out_specs=[pl.BlockSpec((bb, csz, D_BR), lambda i, j: (i, j, 0)), st],
        out_shape=[jax.ShapeDtypeStruct((bsz, t, D_BR), BF16),
                   jax.ShapeDtypeStruct((bsz, N_HEADS, HEAD_DIM, HEAD_DIM), F32)],
        scratch_shapes=[pltpu.VMEM((bb, N_HEADS, HEAD_DIM, HEAD_DIM), F32)],
        compiler_params=_cparams(("parallel", "arbitrary")),
        name="rwkv",
    )(rw, s0, p["r_k"], p["lnx_g"], p["lnx_b"])


def _sb_block(z, causal, tri, carry):
    ls, lk = _log_sigmoids(z)
    if causal is not None:
        lk = jnp.where(causal, lk, 0.0)
    after = jnp.dot(lk.astype(BF16), tri, preferred_element_type=F32)
    a = jnp.exp(ls + after + carry)
    if causal is not None:
        a = jnp.where(causal, a, 0.0)
    return a, carry + after[:, 0:1] + lk[:, 0:1]


def _sba_prompt_kernel(qi_ref, kb_ref, first_ref, last_ref, bias_ref, q_ref, k_ref, v_ref, tri_ref,
                       o_ref, acc, carry, *, tq):
    p = pl.program_id(1)
    first = first_ref[p] == 1
    half = tq // 2

    @pl.when(first)
    def _():
        acc[...] = jnp.zeros_like(acc)
        carry[...] = jnp.zeros_like(carry)

    def compute(diag):
        q = q_ref[0]
        kb = k_ref[0]
        vb = v_ref[0]
        tri = tri_ref[...]
        causal = None
        if diag:
            causal = (lax.broadcasted_iota(jnp.int32, (tq, tq), 1)
                      < lax.broadcasted_iota(jnp.int32, (tq, tq), 0))
        for h in range(N_HEADS):
            sl = slice(h * HEAD_DIM, (h + 1) * HEAD_DIM)
            z = lax.dot_general(q[:, sl], kb[:, sl], (((1,), (1,)), ((), ())),
                                preferred_element_type=F32) + bias_ref[h]
            ls, lk = _log_sigmoids(z)
            if diag:
                lk = jnp.where(causal, lk, 0.0)
            out = None
            c = carry[h]
            for lo in (half, 0):
                lk_h = lk[:, lo:lo + half]
                inner = jnp.dot(lk_h.astype(BF16), tri, preferred_element_type=F32)
                a = jnp.exp(ls[:, lo:lo + half] + inner)
                if diag:
                    a = jnp.where(causal[:, lo:lo + half], a, 0.0)
                part = jnp.exp(c) * jnp.dot(a.astype(BF16), vb[lo:lo + half, sl],
                                            preferred_element_type=F32)
                out = part if out is None else out + part
                c = c + inner[:, 0:1] + lk_h[:, 0:1]
            carry[h] = c
            acc[:, sl] += out

    @pl.when(first)
    def _():
        compute(True)

    @pl.when(jnp.logical_not(first))
    def _():
        compute(False)

    @pl.when(last_ref[p] == 1)
    def _():
        o_ref[0] = acc[...].astype(BF16)


def _sba_prompt_call(q, k, v, bias, tri, tq):
    bsz, t, _ = q.shape
    nq = t // tq
    pairs = [(i, i - j, int(j == 0), int(j == i)) for i in range(nq) for j in range(i + 1)]
    qi_tbl, kb_tbl, first_tbl, last_tbl = (jnp.asarray(c, jnp.int32) for c in zip(*pairs))
    q_map = lambda b, p, qi, kb, fi, la: (b, qi[p], 0)
    kv_map = lambda b, p, qi, kb, fi, la: (b, kb[p], 0)
    grid_spec = pltpu.PrefetchScalarGridSpec(
        num_scalar_prefetch=4,
        grid=(bsz, len(pairs)),
        in_specs=[pl.BlockSpec(memory_space=pltpu.SMEM),
                  pl.BlockSpec((1, tq, D_BR), q_map),
                  pl.BlockSpec((1, tq, D_BR), kv_map),
                  pl.BlockSpec((1, tq, D_BR), kv_map),
                  pl.BlockSpec((tq // 2, tq // 2), lambda b, p, qi, kb, fi, la: (0, 0))],
        out_specs=pl.BlockSpec((1, tq, D_BR), q_map),
        scratch_shapes=[pltpu.VMEM((tq, D_BR), F32), pltpu.VMEM((N_HEADS, tq, 1), F32)],
    )
    return pl.pallas_call(
        functools.partial(_sba_prompt_kernel, tq=tq),
        grid_spec=grid_spec,
        out_shape=jax.ShapeDtypeStruct((bsz, t, D_BR), BF16),
        compiler_params=_cparams(("parallel", "arbitrary")),
        name="sba_prompt",
    )(qi_tbl, kb_tbl, first_tbl, last_tbl, bias, q, k, v, tri)


def _sba_sample_kernel(pt_ref, bias_ref, q_ref, kn_ref, vn_ref, tri_ref, ck_hbm, cv_hbm, o_ref,
                       kbuf, vbuf, sems, acc, carry, *, layer, npg, tpad, nsteps, n_pages, total):
    g = pl.program_id(0)
    s = g & (nsteps - 1)
    slot = g & 1
    rows = N_HEADS * tpad

    def page_copy(step, i, src_hbm, buf, which, page):
        return pltpu.make_async_copy(src_hbm.at[layer, page], buf.at[step & 1, i], sems.at[step & 1, which])

    def start_step(step):
        seq = step >> (nsteps.bit_length() - 1)
        first = n_pages - ((step & (nsteps - 1)) + 1) * npg
        for i in range(npg):
            page = pt_ref[seq, first + i]
            page_copy(step, i, ck_hbm, kbuf, 0, page).start()
            page_copy(step, i, cv_hbm, vbuf, 1, page).start()

    @pl.when(g == 0)
    def _():
        start_step(g)

    @pl.when(g + 1 < total)
    def _():
        start_step(g + 1)

    tri = tri_ref[...]
    tshift = tpad.bit_length() - 1
    rid = lax.broadcasted_iota(jnp.int32, (rows, D_BR), 0) >> tshift
    cid = lax.broadcasted_iota(jnp.int32, (rows, D_BR), 1) >> 6
    head_match = rid == cid
    q = q_ref[0].astype(F32)
    q_bd = jnp.where(head_match, jnp.concatenate([q] * N_HEADS, axis=0), 0.0).astype(BF16)
    hrow = lax.broadcasted_iota(jnp.int32, (rows, 1), 0) >> tshift
    bias = jnp.zeros((rows, 1), F32)
    for h in range(N_HEADS):
        bias = jnp.where(hrow == h, bias_ref[h], bias)

    @pl.when(s == 0)
    def _():
        kpos = lax.broadcasted_iota(jnp.int32, (rows, PAGE), 1)
        qpos = lax.broadcasted_iota(jnp.int32, (rows, PAGE), 0) & (tpad - 1)
        z = lax.dot_general(q_bd, kn_ref[0].astype(BF16), (((1,), (1,)), ((), ())),
                            preferred_element_type=F32) + bias
        a, c_new = _sb_block(z, kpos < qpos, tri, jnp.zeros((rows, 1), F32))
        carry[...] = c_new
        acc[...] = jnp.dot(a.astype(BF16), vn_ref[0].astype(BF16), preferred_element_type=F32)

    for i in range(npg):
        page_copy(g, i, ck_hbm, kbuf, 0, 0).wait()
        page_copy(g, i, cv_hbm, vbuf, 1, 0).wait()
    kt = jnp.concatenate([kbuf[slot, i] for i in range(npg)], axis=1).astype(BF16)
    vt = jnp.concatenate([vbuf[slot, i] for i in range(npg)], axis=1).astype(BF16)
    z = jnp.dot(q_bd, kt, preferred_element_type=F32) + bias
    ls, lk = _log_sigmoids(z)
    page = lambda x, i: x[:, i * PAGE:(i + 1) * PAGE]
    local = [jnp.dot(page(lk, i).astype(BF16), tri, preferred_element_type=F32) for i in range(npg)]
    totals = [local[i][:, 0:1] + page(lk, i)[:, 0:1] for i in range(npg)]
    c = carry[...]
    after = [None] * npg
    for i in range(npg - 1, -1, -1):
        after[i] = local[i] + c
        c = c + totals[i]
    carry[...] = c
    a = jnp.exp(ls + jnp.concatenate(after, axis=1))
    acc[...] += lax.dot_general(a.astype(BF16), vt, (((1,), (1,)), ((), ())), preferred_element_type=F32)

    @pl.when(s == nsteps - 1)
    def _():
        masked = jnp.where(head_match, acc[...], 0.0)
        out = masked[0:tpad]
        for h in range(1, N_HEADS):
            out = out + masked[h * tpad:(h + 1) * tpad]
        o_ref[0] = out.astype(BF16)


def _sba_sample_call(q, kn, vn, cache_k, cache_v, page_table, bias, tri, l, npg):
    bsz, tpad, _ = q.shape
    n_pages = page_table.shape[1]
    nsteps = n_pages // npg
    rows = N_HEADS * tpad
    assert nsteps & (nsteps - 1) == 0 and nsteps * npg == n_pages
    per_seq = lambda r: pl.BlockSpec((1, r, D_BR), lambda g, pt: (g // nsteps, 0, 0))
    grid_spec = pltpu.PrefetchScalarGridSpec(
        num_scalar_prefetch=1,
        grid=(bsz * nsteps,),
        in_specs=[pl.BlockSpec(memory_space=pltpu.SMEM),
                  per_seq(tpad), per_seq(PAGE), per_seq(PAGE),
                  pl.BlockSpec((PAGE, PAGE), lambda g, pt: (0, 0)),
                  pl.BlockSpec(memory_space=pl.ANY), pl.BlockSpec(memory_space=pl.ANY)],
        out_specs=per_seq(tpad),
        scratch_shapes=[pltpu.VMEM((2, npg, D_BR, PAGE), F32), pltpu.VMEM((2, npg, D_BR, PAGE), F32),
                        pltpu.SemaphoreType.DMA((2, 2)),
                        pltpu.VMEM((rows, D_BR), F32), pltpu.VMEM((rows, 1), F32)],
    )
    return pl.pallas_call(
        functools.partial(_sba_sample_kernel, layer=l, npg=npg, tpad=tpad, nsteps=nsteps,
                          n_pages=n_pages, total=bsz * nsteps),
        grid_spec=grid_spec,
        out_shape=jax.ShapeDtypeStruct((bsz, tpad, D_BR), BF16),
        compiler_params=_cparams(("arbitrary",)),
        name="sba_sample",
    )(page_table, bias, q, kn, vn, tri, cache_k, cache_v)


def _merge_kernel(x_ref, mod_ref, u_ref, ya_ref, yb_ref, yc_ref, yd_ref, wg_ref, wb_ref, wo_ref, o_ref):
    x = x_ref[...]
    bb, tm, _ = x.shape
    rows = bb * tm
    ub = u_ref[...].reshape(rows, D_MODEL)
    merged = jnp.zeros((rows, D_MODEL), F32)
    for i, y_ref in enumerate((ya_ref, yb_ref, yc_ref, yd_ref)):
        gate = _sigmoid(jnp.dot(ub, wg_ref[i], preferred_element_type=F32))
        br = jnp.dot(y_ref[...].reshape(rows, D_BR), wb_ref[i], preferred_element_type=F32)
        merged = merged + gate * br
    out = jnp.dot(merged.astype(BF16), wo_ref[...], preferred_element_type=F32)
    g2 = mod_ref[:, 5:6, :]
    o_ref[...] = x + g2 * out.reshape(bb, tm, D_MODEL)


def _merge_call(x, mod, ub, ya, yb, yc, yd, wg, wb, wo, l, bb, tm):
    bsz, t, _ = x.shape
    tok = lambda w: pl.BlockSpec((bb, tm, w), lambda i, j: (i, j, 0))
    return pl.pallas_call(
        _merge_kernel,
        grid=(bsz // bb, t // tm),
        in_specs=[tok(D_MODEL), pl.BlockSpec((bb, N_MOD, D_MODEL), lambda i, j: (i, 0, 0)),
                  tok(D_MODEL), tok(D_BR), tok(D_BR), tok(D_BR), tok(D_BR),
                  _resident((None, N_BRANCH, D_MODEL, D_MODEL), lambda i, j: (l, 0, 0, 0)),
                  _resident((None, N_BRANCH, D_BR, D_MODEL), lambda i, j: (l, 0, 0, 0)),
                  _resident((None, D_MODEL, D_MODEL), lambda i, j: (l, 0, 0))],
        out_specs=tok(D_MODEL),
        out_shape=jax.ShapeDtypeStruct(x.shape, F32),
        compiler_params=_cparams(("parallel", "parallel")),
        name="merge",
    )(x, mod, ub, ya, yb, yc, yd, wg, wb, wo)


def _tri_matrix(n):
    r = lax.broadcasted_iota(jnp.int32, (n, n), 0)
    c = lax.broadcasted_iota(jnp.int32, (n, n), 1)
    return jnp.where(r > c, 1.0, 0.0).astype(BF16)


def kernel(x_prompt, x_sample, c_prompt, c_sample, cache_k, cache_v, page_table, state_conv, state_shift, state_wkv, state_pool, ada_w, ada_b, norm_g, w_ffn_up, w_ffn_down, w_in, conv_w, rwkv_mu, rwkv_w0, rwkv_w2, rwkv_a0, rwkv_a2, rwkv_g2, rwkv_k_k, rwkv_k_a, rwkv_r_k, rwkv_lnx_g, rwkv_lnx_b, pool_w, pool_scale, q_norm_g, k_norm_g, sb_bias, w_gate, w_branch, w_out):
    depth = ada_w.shape[0]
    bp, seq, _ = x_prompt.shape
    bs, dec_seq, _ = x_sample.shape
    n_phys = cache_k.shape[1]
    past_len = page_table.shape[1] * PAGE
    tpad = 8
    tm_p = 512
    tq_p = 512

    wup = w_ffn_up.astype(BF16)
    wdn = w_ffn_down.astype(BF16)
    win = w_in.astype(BF16)
    wg = w_gate.astype(BF16)
    wb = w_branch.astype(BF16)
    wo = w_out.astype(BF16)
    ng = norm_g.reshape(depth * 3, 1, D_MODEL)
    pool_bd = jnp.zeros((depth, len(POOL_WINDOWS), 64, len(POOL_WINDOWS), 64), F32)
    for gi in range(len(POOL_WINDOWS)):
        pool_bd = pool_bd.at[:, gi, :, gi, :].set(pool_w[:, gi])
    prm = {
        "conv_w": conv_w, "mu": rwkv_mu[:, None, :], "w0": rwkv_w0[:, None, :], "w2": rwkv_w2,
        "a0": rwkv_a0[:, None, :], "a2": rwkv_a2, "g2": rwkv_g2, "k_k": rwkv_k_k[:, None, :],
        "k_a": rwkv_k_a[:, None, :], "pool_bd": pool_bd.reshape(depth, D_BR, D_BR).astype(BF16),
        "pool_scale": pool_scale[:, None, :],
        "q_g": jnp.tile(q_norm_g, (1, N_HEADS))[:, None, :],
        "k_g": jnp.tile(k_norm_g, (1, N_HEADS))[:, None, :],
        "r_k": rwkv_r_k[:, None, :], "lnx_g": rwkv_lnx_g[:, None, :], "lnx_b": rwkv_lnx_b[:, None, :],
    }
    ck = jnp.transpose(cache_k, (0, 1, 3, 4, 2)).reshape(depth, n_phys, D_BR, PAGE)
    cv = jnp.transpose(cache_v, (0, 1, 3, 4, 2)).reshape(depth, n_phys, D_BR, PAGE)
    tri_p = _tri_matrix(tq_p // 2)
    tri_s = _tri_matrix(PAGE)

    n_c = bp + bs
    c_all = jnp.concatenate([c_prompt, c_sample, jnp.zeros((-n_c % 8, D_MODEL), F32)], axis=0)
    mod_all = _ada_call(c_all, ada_w, ada_b)
    xs_pad = jnp.concatenate([x_sample, jnp.zeros((bs, tpad - dec_seq, D_MODEL), F32)], axis=1)

    halo_p = jnp.zeros((bp, HALO, EXT_W), F32)
    wkv0_p = jnp.zeros((bp, N_HEADS, HEAD_DIM, HEAD_DIM), F32)

    hp, hs = x_prompt, xs_pad
    outs_p = [[] for _ in range(6)]
    outs_s = [[] for _ in range(6)]
    for l in range(depth):
        mod_p = mod_all[l, :bp].reshape(bp, N_MOD, D_MODEL)
        mod_s = mod_all[l, bp:n_c].reshape(bs, N_MOD, D_MODEL)
        halo_s = jnp.zeros((bs, HALO, EXT_W), F32)
        halo_s = halo_s.at[:, HALO - (CONV_W - 1):, 0:D_BR].set(state_conv[l])
        halo_s = halo_s.at[:, HALO - 1, D_BR:D_BR + D_MODEL].set(state_shift[l])
        halo_s = halo_s.at[:, HALO - POOL_CTX:, D_BR + D_MODEL:].set(state_pool[l])

        for grp in ("p", "s"):
            if grp == "p":
                x, mod, bb, tm, t_real, pos0, halo0, wkv0 = hp, mod_p, 1, tm_p, seq, 0, halo_p, wkv0_p
            else:
                x, mod, bb, tm, t_real, pos0, halo0, wkv0 = hs, mod_s, bs, tpad, dec_seq, past_len, halo_s, state_wkv[l]
            ffn_tile = (2 * tm, 11) if (grp == "p" and seq % (2 * tm) == 0) else (tm, 2)
            x = _ffn_call(x, mod, ng, wup, wdn, l, 0, bb, *ffn_tile)
            ub, ya, yc, rw, q, k, v, k16, v16, conv_n, shift_n, pool_n = _mix_call(
                x, mod, ng, win, halo0, prm, l, bb, tm, t_real, pos0)
            seqs = max(d for d in (1, 2, 4) if rw.shape[0] % d == 0)
            yb, wkv_n = _rwkv_call(rw, wkv0, prm, l, RWKV_CHUNK if grp == "p" else tpad, seqs)
            if grp == "p":
                yd = _sba_prompt_call(q, k16, v16, sb_bias[l], tri_p, tq_p)
            else:
                kn = jnp.concatenate([k, jnp.zeros((bs, PAGE - tpad, D_BR), F32)], axis=1)
                vn = jnp.concatenate([v, jnp.zeros((bs, PAGE - tpad, D_BR), F32)], axis=1)
                yd = _sba_sample_call(q, kn, vn, ck, cv, page_table, sb_bias[l], tri_s, l,
                                      16 if page_table.shape[1] % 16 == 0 else 8)
            x = _merge_call(x, mod, ub, ya, yb, yc, yd, wg, wb, wo, l, bb, tm)
            x = _ffn_call(x, mod, ng, wup, wdn, l, 1, bb, *ffn_tile)
            new = (k[:, :t_real].reshape(-1, t_real, N_HEADS, HEAD_DIM),
                   v[:, :t_real].reshape(-1, t_real, N_HEADS, HEAD_DIM),
                   conv_n, shift_n[:, 0], wkv_n, pool_n)
            if grp == "p":
                hp = x
                for lst, a in zip(outs_p, new):
                    lst.append(a)
            else:
                hs = x
                for lst, a in zip(outs_s, new):
                    lst.append(a)

    stacked_p = [jnp.stack(a, axis=0) for a in outs_p]
    stacked_s = [jnp.stack(a, axis=0) for a in outs_s]
    return (hp, hs[:, :dec_seq], *stacked_p, *stacked_s)
```

```python
import functools

import jax
import jax.numpy as jnp
from jax import lax
from jax.experimental import pallas as pl
from jax.experimental.pallas import tpu as pltpu

F32 = jnp.float32
BF16 = jnp.bfloat16

D_MODEL = 1024
N_BRANCH = 4
D_BR = 256
HEAD_DIM = 64
N_HEADS = 4
CONV_W = 3
POOL_WINDOWS = (2, 4, 8, 16)
POOL_CTX = 15
D_FF = 2816
D_IN = 2816
N_MOD = 9
PAGE = 128
EPS = 1e-6
LNX_EPS = 64e-5

HALO = 16
EXT_W = D_BR + D_MODEL + D_BR
RW_W = 7 * D_BR
RWKV_CHUNK = 64
VMEM_LIMIT = 56 * 1024 * 1024


def _cparams(sem):
    return pltpu.CompilerParams(dimension_semantics=sem, vmem_limit_bytes=VMEM_LIMIT)


def _resident(shape, index_map):
    return pl.BlockSpec(shape, index_map, pipeline_mode=pl.Buffered(1))


def _mm(a, b):
    return jnp.dot(a.astype(BF16), b.astype(BF16), preferred_element_type=F32)


def _mm_nt(a, b):
    return lax.dot_general(a.astype(BF16), b.astype(BF16), (((1,), (1,)), ((), ())),
                           preferred_element_type=F32)


def _mm_tn(a, b):
    return lax.dot_general(a.astype(BF16), b.astype(BF16), (((0,), (0,)), ((), ())),
                           preferred_element_type=F32)


def _split(a):
    hi = a.astype(BF16)
    lo = (a - hi.astype(F32)).astype(BF16)
    return hi, lo


def _mm3(a, b):
    ah, al = _split(a)
    bh, bl = _split(b)
    out = jnp.dot(ah, bh, preferred_element_type=F32)
    out = out + jnp.dot(ah, bl, preferred_element_type=F32)
    return out + jnp.dot(al, bh, preferred_element_type=F32)


def _softplus(y):
    return jnp.maximum(y, 0.0) + jnp.log(1.0 + jnp.exp(-jnp.abs(y)))


def _sigmoid(y):
    return 1.0 / (1.0 + jnp.exp(-y))


def _neg_abs(y):
    bits = lax.bitcast_convert_type(y, jnp.uint32) | jnp.uint32(0x80000000)
    return lax.bitcast_convert_type(bits, F32)


def _log_sigmoids(z):
    ls = jnp.minimum(z, 0.0) - jnp.log(1.0 + jnp.exp(_neg_abs(z)))
    return ls, ls - z


def _head_sum_matrix():
    r = lax.broadcasted_iota(jnp.int32, (D_BR, D_BR), 0) >> 6
    c = lax.broadcasted_iota(jnp.int32, (D_BR, D_BR), 1) >> 6
    return jnp.where(r == c, 1.0, 0.0).astype(BF16)


def _modulated_norm(x, mod_ref, ng_ref, mi):
    sh = mod_ref[:, 3 * mi:3 * mi + 1, :]
    sc = mod_ref[:, 3 * mi + 1:3 * mi + 2, :]
    ms = jnp.mean(x * x, axis=-1, keepdims=True)
    u = x * lax.rsqrt(ms + EPS) * ng_ref[...]
    return u * (1.0 + sc) + sh


def _ada_kernel(c_ref, w_ref, b_ref, o_ref):
    c = c_ref[...]
    s = c * _sigmoid(c)
    o_ref[0] = _mm3(s, w_ref[0]) + b_ref[0]


def _ada_call(c_all, ada_w, ada_b):
    depth = ada_w.shape[0]
    rows = c_all.shape[0]
    n_out = ada_w.shape[2]
    tn = 1152
    return pl.pallas_call(
        _ada_kernel,
        grid=(depth, n_out // tn),
        in_specs=[pl.BlockSpec((rows, D_MODEL), lambda l, j: (0, 0)),
                  pl.BlockSpec((1, D_MODEL, tn), lambda l, j: (l, 0, j)),
                  pl.BlockSpec((1, 1, tn), lambda l, j: (l, 0, j))],
        out_specs=pl.BlockSpec((1, rows, tn), lambda l, j: (l, 0, j)),
        out_shape=jax.ShapeDtypeStruct((depth, rows, n_out), F32),
        compiler_params=_cparams(("parallel", "parallel")),
        name="ada_mod",
    )(c_all, ada_w, ada_b.reshape(depth, 1, n_out))


def _ffn_kernel(x_ref, mod_ref, ng_ref, wup_ref, wdn_ref, o_ref, *, mi, nf):
    x = x_ref[...]
    bb, tm, _ = x.shape
    u = _modulated_norm(x, mod_ref, ng_ref, mi)
    ub = u.reshape(bb * tm, D_MODEL).astype(BF16)
    tf = D_FF // nf
    acc = jnp.zeros((bb * tm, D_MODEL), F32)
    for j in range(nf):
        a = jnp.dot(ub, wup_ref[:, j * tf:(j + 1) * tf], preferred_element_type=F32)
        b = jnp.dot(ub, wup_ref[:, D_FF + j * tf:D_FF + (j + 1) * tf], preferred_element_type=F32)
        h = (a * _sigmoid(a) * b).astype(BF16)
        acc = acc + jnp.dot(h, wdn_ref[j * tf:(j + 1) * tf, :], preferred_element_type=F32)
    g = mod_ref[:, 3 * mi + 2:3 * mi + 3, :]
    o_ref[...] = x + 0.5 * g * acc.reshape(bb, tm, D_MODEL)


def _ffn_call(x, mod, ng, wup, wdn, l, which, bb, tm, nf=2):
    bsz, t, _ = x.shape
    mi = 0 if which == 0 else 2
    kern = functools.partial(_ffn_kernel, mi=mi, nf=nf)
    return pl.pallas_call(
        kern,
        grid=(bsz // bb, t // tm),
        in_specs=[pl.BlockSpec((bb, tm, D_MODEL), lambda i, j: (i, j, 0)),
                  pl.BlockSpec((bb, N_MOD, D_MODEL), lambda i, j: (i, 0, 0)),
                  pl.BlockSpec((None, 1, D_MODEL), lambda i, j: (3 * l + mi, 0, 0)),
                  _resident((None, None, D_MODEL, 2 * D_FF), lambda i, j: (l, which, 0, 0)),
                  _resident((None, None, D_FF, D_MODEL), lambda i, j: (l, which, 0, 0))],
        out_specs=pl.BlockSpec((bb, tm, D_MODEL), lambda i, j: (i, j, 0)),
        out_shape=jax.ShapeDtypeStruct(x.shape, F32),
        compiler_params=_cparams(("parallel", "parallel")),
        name="ffn",
    )(x, mod, ng, wup, wdn)


def _mix_kernel(x_ref, mod_ref, ng_ref, win_ref, halo_ref, convw_ref, mu_ref, w0_ref, ww2_ref,
                a0_ref, wa2_ref, wg2_ref, kk_ref, ka_ref, poolw_ref, pools_ref, qg_ref, kg_ref,
                u_ref, ya_ref, yc_ref, rw_ref, q_ref, k_ref, v_ref, kb_ref, vb_ref,
                conv_ref, shift_ref, pool_ref, ext, *, tm, t_real, pos0, nt):
    t = pl.program_id(1)

    @pl.when(t == 0)
    def _():
        ext[:, 0:HALO, :] = halo_ref[...]

    x = x_ref[...]
    bb = x.shape[0]
    rows = bb * tm
    u = _modulated_norm(x, mod_ref, ng_ref, 1)
    ub = u.reshape(rows, D_MODEL).astype(BF16)
    u_ref[...] = ub.reshape(bb, tm, D_MODEL)
    proj = jnp.dot(ub, win_ref[...], preferred_element_type=F32)

    bg = proj[:, 0:D_BR]
    z = proj[:, D_BR:2 * D_BR] * proj[:, 2 * D_BR:3 * D_BR]
    p_rwkv = proj[:, 3 * D_BR:3 * D_BR + D_MODEL]
    p_pool = proj[:, 3 * D_BR + D_MODEL:4 * D_BR + D_MODEL]
    ext[:, HALO:HALO + tm, 0:D_BR] = z.reshape(bb, tm, D_BR)
    ext[:, HALO:HALO + tm, D_BR:D_BR + D_MODEL] = p_rwkv.reshape(bb, tm, D_MODEL)
    ext[:, HALO:HALO + tm, D_BR + D_MODEL:EXT_W] = p_pool.reshape(bb, tm, D_BR)
    cw = convw_ref[...]
    z1 = ext[:, HALO - 1:HALO - 1 + tm, 0:D_BR].reshape(rows, D_BR)
    z2 = ext[:, HALO - 2:HALO - 2 + tm, 0:D_BR].reshape(rows, D_BR)
    ya = bg * (cw[0:1] * z2 + cw[1:2] * z1 + cw[2:3] * z)
    ya_ref[...] = ya.astype(BF16).reshape(bb, tm, D_BR)

    pp = ext[:, HALO - 1:HALO - 1 + tm, D_BR:D_BR + D_MODEL].reshape(rows, D_MODEL)
    xs = p_rwkv + (pp - p_rwkv) * mu_ref[...]
    r = xs[:, 0:D_BR]
    k = xs[:, D_BR:2 * D_BR]
    v = xs[:, 2 * D_BR:3 * D_BR]
    wd = xs[:, 3 * D_BR:3 * D_BR + 64]
    ad = xs[:, 3 * D_BR + 64:3 * D_BR + 128]
    gd = xs[:, 3 * D_BR + 128:D_MODEL]
    w_log = -_softplus(-(w0_ref[...] + _mm(jnp.tanh(wd), ww2_ref[...]))) - 0.5
    logw = -jnp.exp(w_log)
    a = _sigmoid(a0_ref[...] + _mm(ad, wa2_ref[...]))
    g = _mm(_sigmoid(gd), wg2_ref[...])
    hsum = _head_sum_matrix()
    kkr = k * kk_ref[...]
    kk = kkr / jnp.maximum(jnp.sqrt(_mm(kkr * kkr, hsum)), 1e-12)
    kmod = k * (1.0 + (a - 1.0) * ka_ref[...])
    bvec = kk * a
    tok = t * tm + (lax.broadcasted_iota(jnp.int32, (rows, 1), 0) & (tm - 1))
    if t_real < nt * tm:
        live = jnp.where(tok < t_real, 1.0, 0.0)
        logw, kmod, v, kk, bvec = logw * live, kmod * live, v * live, kk * live, bvec * live
    for i, val in enumerate((r, logw, kmod, v, kk, bvec, g)):
        rw_ref[:, :, i * D_BR:(i + 1) * D_BR] = val.reshape(bb, tm, D_BR)

    col = lax.broadcasted_iota(jnp.int32, (1, D_BR), 1)
    acc = p_pool
    sel = None
    for i in range(1, POOL_WINDOWS[-1]):
        acc = acc + ext[:, HALO - i:HALO - i + tm, D_BR + D_MODEL:EXT_W].reshape(rows, D_BR)
        if (i + 1) in POOL_WINDOWS:
            gi = POOL_WINDOWS.index(i + 1)
            sel = acc if sel is None else jnp.where(col >= gi * 64, acc, sel)
    wcol = jnp.where(col < 64, 2, jnp.where(col < 128, 4, jnp.where(col < 192, 8, 16)))
    cnt = jnp.minimum(wcol, pos0 + tok + 1).astype(F32)
    dd = sel / cnt - p_pool
    yc = _mm(dd, poolw_ref[...]) * pools_ref[...]
    yc_ref[...] = yc.astype(BF16).reshape(bb, tm, D_BR)

    q = proj[:, 4 * D_BR + D_MODEL:5 * D_BR + D_MODEL]
    ks = proj[:, 5 * D_BR + D_MODEL:6 * D_BR + D_MODEL]
    vs = proj[:, 6 * D_BR + D_MODEL:D_IN]
    qn = q * lax.rsqrt(_mm(q * q, hsum) * (1.0 / HEAD_DIM) + EPS) * qg_ref[...]
    kn = ks * lax.rsqrt(_mm(ks * ks, hsum) * (1.0 / HEAD_DIM) + EPS) * kg_ref[...]
    q_ref[...] = (qn * HEAD_DIM ** -0.5).astype(BF16).reshape(bb, tm, D_BR)
    k_ref[...] = kn.reshape(bb, tm, D_BR)
    v_ref[...] = vs.reshape(bb, tm, D_BR)
    kb_ref[...] = kn.astype(BF16).reshape(bb, tm, D_BR)
    vb_ref[...] = vs.astype(BF16).reshape(bb, tm, D_BR)

    @pl.when(t == nt - 1)
    def _():
        last = HALO + t_real - (nt - 1) * tm - 1
        conv_ref[...] = ext[:, last - (CONV_W - 2):last + 1, 0:D_BR]
        shift_ref[...] = ext[:, last:last + 1, D_BR:D_BR + D_MODEL]
        pool_ref[...] = ext[:, last - (POOL_CTX - 1):last + 1, D_BR + D_MODEL:EXT_W]

    if nt > 1:
        ext[:, 0:HALO, :] = ext[:, tm:tm + HALO, :]


def _mix_call(x, mod, ng, win, halo0, p, l, bb, tm, t_real, pos0):
    bsz, t, _ = x.shape
    nt = t // tm
    kern = functools.partial(_mix_kernel, tm=tm, t_real=t_real, pos0=pos0, nt=nt)
    tok = lambda w: pl.BlockSpec((bb, tm, w), lambda i, j: (i, j, 0))
    per_b = lambda r, w: pl.BlockSpec((bb, r, w), lambda i, j: (i, 0, 0))
    row = lambda r, w: pl.BlockSpec((None, r, w), lambda i, j: (l, 0, 0))
    sds = lambda shape, dt: jax.ShapeDtypeStruct(shape, dt)
    return pl.pallas_call(
        kern,
        grid=(bsz // bb, nt),
        in_specs=[tok(D_MODEL), per_b(N_MOD, D_MODEL),
                  pl.BlockSpec((None, 1, D_MODEL), lambda i, j: (3 * l + 1, 0, 0)),
                  _resident((None, D_MODEL, D_IN), lambda i, j: (l, 0, 0)),
                  per_b(HALO, EXT_W),
                  row(CONV_W, D_BR), row(1, D_MODEL), row(1, D_BR), row(64, D_BR), row(1, D_BR),
                  row(64, D_BR), row(128, D_BR), row(1, D_BR), row(1, D_BR), row(D_BR, D_BR),
                  row(1, D_BR), row(1, D_BR), row(1, D_BR)],
        out_specs=[tok(D_MODEL), tok(D_BR), tok(D_BR), tok(RW_W), tok(D_BR), tok(D_BR), tok(D_BR),
                   tok(D_BR), tok(D_BR),
                   per_b(CONV_W - 1, D_BR), per_b(1, D_MODEL), per_b(POOL_CTX, D_BR)],
        out_shape=[sds((bsz, t, D_MODEL), BF16), sds((bsz, t, D_BR), BF16), sds((bsz, t, D_BR), BF16),
                   sds((bsz, t, RW_W), F32), sds((bsz, t, D_BR), BF16), sds((bsz, t, D_BR), F32),
                   sds((bsz, t, D_BR), F32), sds((bsz, t, D_BR), BF16), sds((bsz, t, D_BR), BF16),
                   sds((bsz, CONV_W - 1, D_BR), F32),
                   sds((bsz, 1, D_MODEL), F32), sds((bsz, POOL_CTX, D_BR), F32)],
        scratch_shapes=[pltpu.VMEM((bb, HALO + tm, EXT_W), F32)],
        compiler_params=_cparams(("parallel", "arbitrary")),
        name="mix_prep",
    )(x, mod, ng, win, halo0, p["conv_w"], p["mu"], p["w0"], p["w2"], p["a0"], p["a2"], p["g2"],
      p["k_k"], p["k_a"], p["pool_bd"], p["pool_scale"], p["q_g"], p["k_g"])


def _rwkv_kernel(rw_ref, s0_ref, rk_ref, lng_ref, lnb_ref, y_ref, sout_ref, s_scr, *, nc, bb):
    c_idx = pl.program_id(1)

    @pl.when(c_idx == 0)
    def _():
        s_scr[...] = s0_ref[...]

    c_in = rw_ref.shape[1]
    csz = RWKV_CHUNK
    ri = lax.broadcasted_iota(jnp.int32, (csz, csz), 0)
    ci = lax.broadcasted_iota(jnp.int32, (csz, csz), 1)
    lower = ri >= ci
    strict = ri > ci
    tril = jnp.where(lower, 1.0, 0.0).astype(BF16)
    eye = jnp.where(ri == ci, 1.0, 0.0)
    levels = [((ri >> (s + 1)) == (ci >> (s + 1))) & ((ri >> s) != (ci >> s))
              for s in range(csz.bit_length() - 1)]

    probs = [(b, h) for b in range(bb) for h in range(N_HEADS)]
    each = lambda fn, *lists: [fn(*args) for args in zip(*lists)]

    rws, cums = [], []
    for b in range(bb):
        rw = rw_ref[b]
        if c_in < csz:
            rw = jnp.concatenate([rw, jnp.zeros((csz - c_in, RW_W), F32)], axis=0)
        lh, ll = _split(rw[:, D_BR:2 * D_BR])
        cums.append(jnp.dot(tril, lh, preferred_element_type=F32)
                    + jnp.dot(tril, ll, preferred_element_type=F32))
        rws.append(rw)

    def operands(b, h):
        lo = h * HEAD_DIM
        r, logw, k, v, kk, bv, g = (rws[b][:, i * D_BR + lo:i * D_BR + lo + HEAD_DIM] for i in range(7))
        cum = cums[b][:, lo:lo + HEAD_DIM]
        tot = cum[csz - 1:csz, :]
        e_inv = jnp.exp(-cum)
        e_rest = jnp.exp(tot - cum)
        return dict(r=r, k=k, v=v, g=g, kt=kk * jnp.exp(cum - logw), bt=bv * e_inv, kd=k * e_inv,
                    rt=r * jnp.exp(cum), bh=bv * e_rest, kh=k * e_rest, gam=jnp.exp(tot))

    ops = [operands(b, h) for b, h in probs]
    get = lambda name: [o[name] for o in ops]
    kt, bt, kd, rt, v = get("kt"), get("bt"), get("kd"), get("rt"), get("v")

    kr = each(lambda a, c: jnp.concatenate([a, c], axis=0), kt, rt)
    pb = each(_mm_nt, kr, bt)
    pk = each(_mm_nt, kr, kd)
    a_b = [jnp.where(strict, p[0:csz], 0.0) for p in pb]
    m_b = [jnp.where(lower, p[csz:2 * csz], 0.0) for p in pb]
    a_k = [jnp.where(strict, p[0:csz], 0.0) for p in pk]
    m_k = [jnp.where(lower, p[csz:2 * csz], 0.0) for p in pk]

    tinv = [eye - jnp.where(levels[0], a, 0.0) for a in a_b]
    for lv in levels[1:]:
        x = each(lambda a, t: _mm(jnp.where(lv, a, 0.0), t), a_b, tinv)
        tinv = each(lambda t, xx: t - _mm(t, xx), tinv, x)

    akv = each(_mm, a_k, v)
    w1 = each(_mm, tinv, kt)
    w2 = each(_mm, tinv, akv)
    qt = each(lambda r_, m, w: r_ - _mm(m, w), rt, m_b, w1)
    y0 = each(lambda mk, vv, mb, w: _mm(mk, vv) - _mm(mb, w), m_k, v, m_b, w2)
    m_neg = each(_mm_tn, w1, get("bh"))
    n_add = each(lambda vv, w, kh_, bh_: _mm_tn(jnp.concatenate([vv, w], axis=0),
                                                jnp.concatenate([kh_, -bh_], axis=0)),
                 v, w2, get("kh"), get("bh"))

    s_prev = [s_scr[b, h] for b, h in probs]
    y = each(lambda q_, s_, y_: _mm_nt(q_, s_) + y_, qt, s_prev, y0)
    s_new = each(lambda s_, o, mn, na: s_ * o["gam"] - _mm(s_, mn) + na, s_prev, ops, m_neg, n_add)
    for (b, h), s_ in zip(probs, s_new):
        s_scr[b, h] = s_

    for (b, h), o, y_ in zip(probs, ops, y):
        sl = slice(h * HEAD_DIM, (h + 1) * HEAD_DIM)
        mean = jnp.mean(y_, axis=-1, keepdims=True)
        yc = y_ - mean
        var = jnp.mean(yc * yc, axis=-1, keepdims=True)
        yn = yc * lax.rsqrt(var + LNX_EPS) * lng_ref[:, sl] + lnb_ref[:, sl]
        bonus = jnp.sum(o["r"] * o["k"] * rk_ref[:, sl], axis=-1, keepdims=True) * o["v"]
        y_ref[b, :, sl] = ((yn + bonus) * o["g"])[0:c_in].astype(BF16)

    @pl.when(c_idx == nc - 1)
    def _():
        sout_ref[...] = s_scr[...]


def _rwkv_call(rw, s0, p, l, csz, bb):
    bsz, t, _ = rw.shape
    nc = t // csz
    row = lambda: pl.BlockSpec((None, 1, D_BR), lambda i, j: (l, 0, 0))
    st = pl.BlockSpec((bb, N_HEADS, HEAD_DIM, HEAD_DIM), lambda i, j: (i, 0, 0, 0))
    return pl.pallas_call(
        functools.partial(_rwkv_kernel, nc=nc, bb=bb),
        grid=(bsz // bb, nc),
        in_specs=[pl.BlockSpec((bb, csz, RW_W), lambda i, j: (i, j, 0)), st, row(), row(), row()],
        out_specs=[pl.BlockSpec((bb, csz, D_BR), lambda i, j: (i, j, 0)), st],
        out_shape=[jax.ShapeDtypeStruct((bsz, t, D_BR), BF16),
                   jax.ShapeDtypeStruct((bsz, N_HEADS, HEAD_DIM, HEAD_DIM), F32)],
        scratch_shapes=[pltpu.VMEM((bb, N_HEADS, HEAD_DIM, HEAD_DIM), F32)],
        compiler_params=_cparams(("parallel", "arbitrary")),
        name="rwkv",
    )(rw, s0, p["r_k"], p["lnx_g"], p["lnx_b"])


def _sb_block(z, causal, tri, carry):
    ls, lk = _log_sigmoids(z)
    if causal is not None:
        lk = jnp.where(causal, lk, 0.0)
    after = jnp.dot(lk.astype(BF16), tri, preferred_element_type=F32)
    a = jnp.exp(ls + after + carry)
    if causal is not None:
        a = jnp.where(causal, a, 0.0)
    return a, carry + after[:, 0:1] + lk[:, 0:1]


def _sba_prompt_kernel(qi_ref, kb_ref, first_ref, last_ref, bias_ref, q_ref, k_ref, v_ref, tri_ref,
                       o_ref, acc, carry, *, tq):
    p = pl.program_id(1)
    first = first_ref[p] == 1
    half = tq // 2

    @pl.when(first)
    def _():
        acc[...] = jnp.zeros_like(acc)
        carry[...] = jnp.zeros_like(carry)

    def compute(diag):
        q = q_ref[0]
        kb = k_ref[0]
        vb = v_ref[0]
        tri = tri_ref[...]
        causal = None
        if diag:
            causal = (lax.broadcasted_iota(jnp.int32, (tq, tq), 1)
                      < lax.broadcasted_iota(jnp.int32, (tq, tq), 0))
        for h in range(N_HEADS):
            sl = slice(h * HEAD_DIM, (h + 1) * HEAD_DIM)
            out = None
            c = carry[h]
            for lo in (half, 0):
                r0 = half if (diag and lo == half) else 0
                z = lax.dot_general(q[r0:, sl], kb[lo:lo + half, sl], (((1,), (1,)), ((), ())),
                                    preferred_element_type=F32) + bias_ref[h]
                ls, lk = _log_sigmoids(z)
                if diag:
                    seen = causal[r0:, lo:lo + half]
                    lk = jnp.where(seen, lk, 0.0)
                inner = jnp.dot(lk.astype(BF16), tri, preferred_element_type=F32)
                a = jnp.exp(ls + inner)
                if diag:
                    a = jnp.where(seen, a, 0.0)
                part = jnp.exp(c[r0:]) * jnp.dot(a.astype(BF16), vb[lo:lo + half, sl],
                                                 preferred_element_type=F32)
                total = inner[:, 0:1] + lk[:, 0:1]
                if r0:
                    part = jnp.concatenate([jnp.zeros((r0, HEAD_DIM), F32), part], axis=0)
                    total = jnp.concatenate([jnp.zeros((r0, 1), F32), total], axis=0)
                out = part if out is None else out + part
                c = c + total
            carry[h] = c
            acc[:, sl] += out

    @pl.when(first)
    def _():
        compute(True)

    @pl.when(jnp.logical_not(first))
    def _():
        compute(False)

    @pl.when(last_ref[p] == 1)
    def _():
        o_ref[0] = acc[...].astype(BF16)


def _sba_prompt_call(q, k, v, bias, tri, tq):
    bsz, t, _ = q.shape
    nq = t // tq
    pairs = [(i, i - j, int(j == 0), int(j == i)) for i in range(nq) for j in range(i + 1)]
    qi_tbl, kb_tbl, first_tbl, last_tbl = (jnp.asarray(c, jnp.int32) for c in zip(*pairs))
    q_map = lambda b, p, qi, kb, fi, la: (b, qi[p], 0)
    kv_map = lambda b, p, qi, kb, fi, la: (b, kb[p], 0)
    grid_spec = pltpu.PrefetchScalarGridSpec(
        num_scalar_prefetch=4,
        grid=(bsz, len(pairs)),
        in_specs=[pl.BlockSpec(memory_space=pltpu.SMEM),
                  pl.BlockSpec((1, tq, D_BR), q_map),
                  pl.BlockSpec((1, tq, D_BR), kv_map),
                  pl.BlockSpec((1, tq, D_BR), kv_map),
                  pl.BlockSpec((tq // 2, tq // 2), lambda b, p, qi, kb, fi, la: (0, 0))],
        out_specs=pl.BlockSpec((1, tq, D_BR), q_map),
        scratch_shapes=[pltpu.VMEM((tq, D_BR), F32), pltpu.VMEM((N_HEADS, tq, 1), F32)],
    )
    return pl.pallas_call(
        functools.partial(_sba_prompt_kernel, tq=tq),
        grid_spec=grid_spec,
        out_shape=jax.ShapeDtypeStruct((bsz, t, D_BR), BF16),
        compiler_params=_cparams(("parallel", "arbitrary")),
        name="sba_prompt",
    )(qi_tbl, kb_tbl, first_tbl, last_tbl, bias, q, k, v, tri)


def _sba_sample_kernel(pt_ref, bias_ref, q_ref, kn_ref, vn_ref, tri_ref, ck_hbm, cv_hbm, o_ref,
                       kbuf, vbuf, sems, acc, carry, *, layer, npg, tpad, nsteps, n_pages, total):
    g = pl.program_id(0)
    s = g & (nsteps - 1)
    slot = g & 1
    rows = N_HEADS * tpad

    def page_copy(step, i, src_hbm, buf, which, page):
        return pltpu.make_async_copy(src_hbm.at[layer, page], buf.at[step & 1, i], sems.at[step & 1, which])

    def start_step(step):
        seq = step >> (nsteps.bit_length() - 1)
        first = n_pages - ((step & (nsteps - 1)) + 1) * npg
        for i in range(npg):
            page = pt_ref[seq, first + i]
            page_copy(step, i, ck_hbm, kbuf, 0, page).start()
            page_copy(step, i, cv_hbm, vbuf, 1, page).start()

    @pl.when(g == 0)
    def _():
        start_step(g)

    @pl.when(g + 1 < total)
    def _():
        start_step(g + 1)

    tri = tri_ref[...]
    tshift = tpad.bit_length() - 1
    rid = lax.broadcasted_iota(jnp.int32, (rows, D_BR), 0) >> tshift
    cid = lax.broadcasted_iota(jnp.int32, (rows, D_BR), 1) >> 6
    head_match = rid == cid
    q = q_ref[0].astype(F32)
    q_bd = jnp.where(head_match, jnp.concatenate([q] * N_HEADS, axis=0), 0.0).astype(BF16)
    hrow = lax.broadcasted_iota(jnp.int32, (rows, 1), 0) >> tshift
    bias = jnp.zeros((rows, 1), F32)
    for h in range(N_HEADS):
        bias = jnp.where(hrow == h, bias_ref[h], bias)

    @pl.when(s == 0)
    def _():
        kpos = lax.broadcasted_iota(jnp.int32, (rows, PAGE), 1)
        qpos = lax.broadcasted_iota(jnp.int32, (rows, PAGE), 0) & (tpad - 1)
        z = lax.dot_general(q_bd, kn_ref[0].astype(BF16), (((1,), (1,)), ((), ())),
                            preferred_element_type=F32) + bias
        a, c_new = _sb_block(z, kpos < qpos, tri, jnp.zeros((rows, 1), F32))
        carry[...] = c_new
        acc[...] = jnp.dot(a.astype(BF16), vn_ref[0].astype(BF16), preferred_element_type=F32)

    for i in range(npg):
        page_copy(g, i, ck_hbm, kbuf, 0, 0).wait()
        page_copy(g, i, cv_hbm, vbuf, 1, 0).wait()
    kt = jnp.concatenate([kbuf[slot, i] for i in range(npg)], axis=1).astype(BF16)
    vt = jnp.concatenate([vbuf[slot, i] for i in range(npg)], axis=1).astype(BF16)
    z = jnp.dot(q_bd, kt, preferred_element_type=F32) + bias
    ls, lk = _log_sigmoids(z)
    page = lambda x, i: x[:, i * PAGE:(i + 1) * PAGE]
    local = [jnp.dot(page(lk, i).astype(BF16), tri, preferred_element_type=F32) for i in range(npg)]
    totals = [local[i][:, 0:1] + page(lk, i)[:, 0:1] for i in range(npg)]
    c = carry[...]
    after = [None] * npg
    for i in range(npg - 1, -1, -1):
        after[i] = local[i] + c
        c = c + totals[i]
    carry[...] = c
    a = jnp.exp(ls + jnp.concatenate(after, axis=1))
    acc[...] += lax.dot_general(a.astype(BF16), vt, (((1,), (1,)), ((), ())), preferred_element_type=F32)

    @pl.when(s == nsteps - 1)
    def _():
        masked = jnp.where(head_match, acc[...], 0.0)
        out = masked[0:tpad]
        for h in range(1, N_HEADS):
            out = out + masked[h * tpad:(h + 1) * tpad]
        o_ref[0] = out.astype(BF16)


def _sba_sample_call(q, kn, vn, cache_k, cache_v, page_table, bias, tri, l, npg):
    bsz, tpad, _ = q.shape
    n_pages = page_table.shape[1]
    nsteps = n_pages // npg
    rows = N_HEADS * tpad
    assert nsteps & (nsteps - 1) == 0 and nsteps * npg == n_pages
    per_seq = lambda r: pl.BlockSpec((1, r, D_BR), lambda g, pt: (g // nsteps, 0, 0))
    grid_spec = pltpu.PrefetchScalarGridSpec(
        num_scalar_prefetch=1,
        grid=(bsz * nsteps,),
        in_specs=[pl.BlockSpec(memory_space=pltpu.SMEM),
                  per_seq(tpad), per_seq(PAGE), per_seq(PAGE),
                  pl.BlockSpec((PAGE, PAGE), lambda g, pt: (0, 0)),
                  pl.BlockSpec(memory_space=pl.ANY), pl.BlockSpec(memory_space=pl.ANY)],
        out_specs=per_seq(tpad),
        scratch_shapes=[pltpu.VMEM((2, npg, D_BR, PAGE), F32), pltpu.VMEM((2, npg, D_BR, PAGE), F32),
                        pltpu.SemaphoreType.DMA((2, 2)),
                        pltpu.VMEM((rows, D_BR), F32), pltpu.VMEM((rows, 1), F32)],
    )
    return pl.pallas_call(
        functools.partial(_sba_sample_kernel, layer=l, npg=npg, tpad=tpad, nsteps=nsteps,
                          n_pages=n_pages, total=bsz * nsteps),
        grid_spec=grid_spec,
        out_shape=jax.ShapeDtypeStruct((bsz, tpad, D_BR), BF16),
        compiler_params=_cparams(("arbitrary",)),
        name="sba_sample",
    )(page_table, bias, q, kn, vn, tri, cache_k, cache_v)


def _merge_kernel(x_ref, mod_ref, u_ref, ya_ref, yb_ref, yc_ref, yd_ref, wg_ref, wb_ref, wo_ref, o_ref):
    x = x_ref[...]
    bb, tm, _ = x.shape
    rows = bb * tm
    ub = u_ref[...].reshape(rows, D_MODEL)
    ys = [r[...].reshape(rows, D_BR) for r in (ya_ref, yb_ref, yc_ref, yd_ref)]
    out = jnp.zeros((rows, D_MODEL), F32)
    for c in range(D_MODEL // D_BR):
        cols = slice(c * D_BR, (c + 1) * D_BR)
        merged = jnp.zeros((rows, D_BR), F32)
        for i, y in enumerate(ys):
            gate = _sigmoid(jnp.dot(ub, wg_ref[i, :, cols], preferred_element_type=F32))
            merged = merged + gate * jnp.dot(y, wb_ref[i, :, cols], preferred_element_type=F32)
        out = out + jnp.dot(merged.astype(BF16), wo_ref[cols, :], preferred_element_type=F32)
    g2 = mod_ref[:, 5:6, :]
    o_ref[...] = x + g2 * out.reshape(bb, tm, D_MODEL)


def _merge_call(x, mod, ub, ya, yb, yc, yd, wg, wb, wo, l, bb, tm):
    bsz, t, _ = x.shape
    tok = lambda w: pl.BlockSpec((bb, tm, w), lambda i, j: (i, j, 0))
    return pl.pallas_call(
        _merge_kernel,
        grid=(bsz // bb, t // tm),
        in_specs=[tok(D_MODEL), pl.BlockSpec((bb, N_MOD, D_MODEL), lambda i, j: (i, 0, 0)),
                  tok(D_MODEL), tok(D_BR), tok(D_BR), tok(D_BR), tok(D_BR),
                  _resident((None, N_BRANCH, D_MODEL, D_MODEL), lambda i, j: (l, 0, 0, 0)),
                  _resident((None, N_BRANCH, D_BR, D_MODEL), lambda i, j: (l, 0, 0, 0)),
                  _resident((None, D_MODEL, D_MODEL), lambda i, j: (l, 0, 0))],
        out_specs=tok(D_MODEL),
        out_shape=jax.ShapeDtypeStruct(x.shape, F32),
        compiler_params=_cparams(("parallel", "parallel")),
        name="merge",
    )(x, mod, ub, ya, yb, yc, yd, wg, wb, wo)


def _tri_matrix(n):
    r = lax.broadcasted_iota(jnp.int32, (n, n), 0)
    c = lax.broadcasted_iota(jnp.int32, (n, n), 1)
    return jnp.where(r > c, 1.0, 0.0).astype(BF16)


def kernel(x_prompt, x_sample, c_prompt, c_sample, cache_k, cache_v, page_table, state_conv, state_shift, state_wkv, state_pool, ada_w, ada_b, norm_g, w_ffn_up, w_ffn_down, w_in, conv_w, rwkv_mu, rwkv_w0, rwkv_w2, rwkv_a0, rwkv_a2, rwkv_g2, rwkv_k_k, rwkv_k_a, rwkv_r_k, rwkv_lnx_g, rwkv_lnx_b, pool_w, pool_scale, q_norm_g, k_norm_g, sb_bias, w_gate, w_branch, w_out):
    depth = ada_w.shape[0]
    bp, seq, _ = x_prompt.shape
    bs, dec_seq, _ = x_sample.shape
    n_phys = cache_k.shape[1]
    past_len = page_table.shape[1] * PAGE
    tpad = 8
    tm_p = 512
    tq_p = 512

    wup = w_ffn_up.astype(BF16)
    wdn = w_ffn_down.astype(BF16)
    win = w_in.astype(BF16)
    wg = w_gate.astype(BF16)
    wb = w_branch.astype(BF16)
    wo = w_out.astype(BF16)
    ng = norm_g.reshape(depth * 3, 1, D_MODEL)
    pool_bd = jnp.zeros((depth, len(POOL_WINDOWS), 64, len(POOL_WINDOWS), 64), F32)
    for gi in range(len(POOL_WINDOWS)):
        pool_bd = pool_bd.at[:, gi, :, gi, :].set(pool_w[:, gi])
    prm = {
        "conv_w": conv_w, "mu": rwkv_mu[:, None, :], "w0": rwkv_w0[:, None, :], "w2": rwkv_w2,
        "a0": rwkv_a0[:, None, :], "a2": rwkv_a2, "g2": rwkv_g2, "k_k": rwkv_k_k[:, None, :],
        "k_a": rwkv_k_a[:, None, :], "pool_bd": pool_bd.reshape(depth, D_BR, D_BR).astype(BF16),
        "pool_scale": pool_scale[:, None, :],
        "q_g": jnp.tile(q_norm_g, (1, N_HEADS))[:, None, :],
        "k_g": jnp.tile(k_norm_g, (1, N_HEADS))[:, None, :],
        "r_k": rwkv_r_k[:, None, :], "lnx_g": rwkv_lnx_g[:, None, :], "lnx_b": rwkv_lnx_b[:, None, :],
    }
    ck = jnp.transpose(cache_k, (0, 1, 3, 4, 2)).reshape(depth, n_phys, D_BR, PAGE)
    cv = jnp.transpose(cache_v, (0, 1, 3, 4, 2)).reshape(depth, n_phys, D_BR, PAGE)
    tri_p = _tri_matrix(tq_p // 2)
    tri_s = _tri_matrix(PAGE)

    n_c = bp + bs
    c_all = jnp.concatenate([c_prompt, c_sample, jnp.zeros((-n_c % 8, D_MODEL), F32)], axis=0)
    mod_all = _ada_call(c_all, ada_w, ada_b)
    xs_pad = jnp.concatenate([x_sample, jnp.zeros((bs, tpad - dec_seq, D_MODEL), F32)], axis=1)

    halo_p = jnp.zeros((bp, HALO, EXT_W), F32)
    wkv0_p = jnp.zeros((bp, N_HEADS, HEAD_DIM, HEAD_DIM), F32)

    hp, hs = x_prompt, xs_pad
    outs_p = [[] for _ in range(6)]
    outs_s = [[] for _ in range(6)]
    for l in range(depth):
        mod_p = mod_all[l, :bp].reshape(bp, N_MOD, D_MODEL)
        mod_s = mod_all[l, bp:n_c].reshape(bs, N_MOD, D_MODEL)
        halo_s = jnp.zeros((bs, HALO, EXT_W), F32)
        halo_s = halo_s.at[:, HALO - (CONV_W - 1):, 0:D_BR].set(state_conv[l])
        halo_s = halo_s.at[:, HALO - 1, D_BR:D_BR + D_MODEL].set(state_shift[l])
        halo_s = halo_s.at[:, HALO - POOL_CTX:, D_BR + D_MODEL:].set(state_pool[l])

        for grp in ("p", "s"):
            if grp == "p":
                x, mod, bb, tm, t_real, pos0, halo0, wkv0 = hp, mod_p, 1, tm_p, seq, 0, halo_p, wkv0_p
            else:
                x, mod, bb, tm, t_real, pos0, halo0, wkv0 = hs, mod_s, bs, tpad, dec_seq, past_len, halo_s, state_wkv[l]
            ffn_tile = (2 * tm, 11) if (grp == "p" and seq % (2 * tm) == 0) else (tm, 2)
            x = _ffn_call(x, mod, ng, wup, wdn, l, 0, bb, *ffn_tile)
            ub, ya, yc, rw, q, k, v, k16, v16, conv_n, shift_n, pool_n = _mix_call(
                x, mod, ng, win, halo0, prm, l, bb, tm, t_real, pos0)
            seqs = max(d for d in (1, 2, 4) if rw.shape[0] % d == 0)
            yb, wkv_n = _rwkv_call(rw, wkv0, prm, l, RWKV_CHUNK if grp == "p" else tpad, seqs)
            if grp == "p":
                yd = _sba_prompt_call(q, k16, v16, sb_bias[l], tri_p, tq_p)
            else:
                kn = jnp.concatenate([k, jnp.zeros((bs, PAGE - tpad, D_BR), F32)], axis=1)
                vn = jnp.concatenate([v, jnp.zeros((bs, PAGE - tpad, D_BR), F32)], axis=1)
                yd = _sba_sample_call(q, kn, vn, ck, cv, page_table, sb_bias[l], tri_s, l,
                                      16 if page_table.shape[1] % 16 == 0 else 8)
            x = _merge_call(x, mod, ub, ya, yb, yc, yd, wg, wb, wo, l, bb, ffn_tile[0])
            x = _ffn_call(x, mod, ng, wup, wdn, l, 1, bb, *ffn_tile)
            new = (k[:, :t_real].reshape(-1, t_real, N_HEADS, HEAD_DIM),
                   v[:, :t_real].reshape(-1, t_real, N_HEADS, HEAD_DIM),
                   conv_n, shift_n[:, 0], wkv_n, pool_n)
            if grp == "p":
                hp = x
                for lst, a in zip(outs_p, new):
                    lst.append(a)
            else:
                hs = x
                for lst, a in zip(outs_s, new):
                    lst.append(a)

    stacked_p = [jnp.stack(a, axis=0) for a in outs_p]
    stacked_s = [jnp.stack(a, axis=0) for a in outs_s]
    return (hp, hs[:, :dec_seq], *stacked_p, *stacked_s)
```

```python
import functools

import jax
import jax.numpy as jnp
from jax import lax
from jax.experimental import pallas as pl
from jax.experimental.pallas import tpu as pltpu

F32 = jnp.float32
BF16 = jnp.bfloat16

D_MODEL = 1024
N_BRANCH = 4
D_BR = 256
HEAD_DIM = 64
N_HEADS = 4
CONV_W = 3
POOL_WINDOWS = (2, 4, 8, 16)
POOL_CTX = 15
D_FF = 2816
D_IN = 2816
N_MOD = 9
PAGE = 128
EPS = 1e-6
LNX_EPS = 64e-5

HALO = 16
EXT_W = D_BR + D_MODEL + D_BR
RW_W = 7 * D_BR
RWKV_CHUNK = 64
VMEM_LIMIT = 56 * 1024 * 1024


def _cparams(sem):
    return pltpu.CompilerParams(dimension_semantics=sem, vmem_limit_bytes=VMEM_LIMIT)


def _resident(shape, index_map):
    return pl.BlockSpec(shape, index_map, pipeline_mode=pl.Buffered(1))


def _mm(a, b):
    return jnp.dot(a.astype(BF16), b.astype(BF16), preferred_element_type=F32)


def _mm_nt(a, b):
    return lax.dot_general(a.astype(BF16), b.astype(BF16), (((1,), (1,)), ((), ())),
                           preferred_element_type=F32)


def _mm_tn(a, b):
    return lax.dot_general(a.astype(BF16), b.astype(BF16), (((0,), (0,)), ((), ())),
                           preferred_element_type=F32)


def _split(a):
    hi = a.astype(BF16)
    lo = (a - hi.astype(F32)).astype(BF16)
    return hi, lo


def _mm3(a, b):
    ah, al = _split(a)
    bh, bl = _split(b)
    out = jnp.dot(ah, bh, preferred_element_type=F32)
    out = out + jnp.dot(ah, bl, preferred_element_type=F32)
    return out + jnp.dot(al, bh, preferred_element_type=F32)


def _softplus(y):
    return jnp.maximum(y, 0.0) + jnp.log(1.0 + jnp.exp(-jnp.abs(y)))


def _sigmoid(y):
    return 1.0 / (1.0 + jnp.exp(-y))


def _neg_abs(y):
    bits = lax.bitcast_convert_type(y, jnp.uint32) | jnp.uint32(0x80000000)
    return lax.bitcast_convert_type(bits, F32)


def _log_sigmoids(z):
    ls = jnp.minimum(z, 0.0) - jnp.log(1.0 + jnp.exp(_neg_abs(z)))
    return ls, ls - z


def _head_sum_matrix():
    r = lax.broadcasted_iota(jnp.int32, (D_BR, D_BR), 0) >> 6
    c = lax.broadcasted_iota(jnp.int32, (D_BR, D_BR), 1) >> 6
    return jnp.where(r == c, 1.0, 0.0).astype(BF16)


def _modulated_norm(x, mod_ref, ng_ref, mi):
    sh = mod_ref[:, 3 * mi:3 * mi + 1, :]
    sc = mod_ref[:, 3 * mi + 1:3 * mi + 2, :]
    ms = jnp.mean(x * x, axis=-1, keepdims=True)
    u = x * lax.rsqrt(ms + EPS) * ng_ref[...]
    return u * (1.0 + sc) + sh


def _ada_kernel(c_ref, w_ref, b_ref, o_ref):
    c = c_ref[...]
    s = c * _sigmoid(c)
    o_ref[0] = _mm3(s, w_ref[0]) + b_ref[0]


def _ada_call(c_all, ada_w, ada_b):
    depth = ada_w.shape[0]
    rows = c_all.shape[0]
    n_out = ada_w.shape[2]
    tn = 1152
    return pl.pallas_call(
        _ada_kernel,
        grid=(depth, n_out // tn),
        in_specs=[pl.BlockSpec((rows, D_MODEL), lambda l, j: (0, 0)),
                  pl.BlockSpec((1, D_MODEL, tn), lambda l, j: (l, 0, j)),
                  pl.BlockSpec((1, 1, tn), lambda l, j: (l, 0, j))],
        out_specs=pl.BlockSpec((1, rows, tn), lambda l, j: (l, 0, j)),
        out_shape=jax.ShapeDtypeStruct((depth, rows, n_out), F32),
        compiler_params=_cparams(("parallel", "parallel")),
        name="ada_mod",
    )(c_all, ada_w, ada_b.reshape(depth, 1, n_out))


def _ffn_kernel(x_ref, mod_ref, ng_ref, wup_ref, wdn_ref, o_ref, *, mi, nf):
    x = x_ref[...]
    bb, tm, _ = x.shape
    u = _modulated_norm(x, mod_ref, ng_ref, mi)
    ub = u.reshape(bb * tm, D_MODEL).astype(BF16)
    tf = D_FF // nf
    acc = jnp.zeros((bb * tm, D_MODEL), F32)
    for j in range(nf):
        a = jnp.dot(ub, wup_ref[:, j * tf:(j + 1) * tf], preferred_element_type=F32)
        b = jnp.dot(ub, wup_ref[:, D_FF + j * tf:D_FF + (j + 1) * tf], preferred_element_type=F32)
        h = (a * _sigmoid(a) * b).astype(BF16)
        acc = acc + jnp.dot(h, wdn_ref[j * tf:(j + 1) * tf, :], preferred_element_type=F32)
    g = mod_ref[:, 3 * mi + 2:3 * mi + 3, :]
    o_ref[...] = x + 0.5 * g * acc.reshape(bb, tm, D_MODEL)


def _ffn_call(x, mod, ng, wup, wdn, l, which, bb, tm, nf=2):
    bsz, t, _ = x.shape
    mi = 0 if which == 0 else 2
    kern = functools.partial(_ffn_kernel, mi=mi, nf=nf)
    return pl.pallas_call(
        kern,
        grid=(bsz // bb, t // tm),
        in_specs=[pl.BlockSpec((bb, tm, D_MODEL), lambda i, j: (i, j, 0)),
                  pl.BlockSpec((bb, N_MOD, D_MODEL), lambda i, j: (i, 0, 0)),
                  pl.BlockSpec((None, 1, D_MODEL), lambda i, j: (3 * l + mi, 0, 0)),
                  _resident((None, None, D_MODEL, 2 * D_FF), lambda i, j: (l, which, 0, 0)),
                  _resident((None, None, D_FF, D_MODEL), lambda i, j: (l, which, 0, 0))],
        out_specs=pl.BlockSpec((bb, tm, D_MODEL), lambda i, j: (i, j, 0)),
        out_shape=jax.ShapeDtypeStruct(x.shape, F32),
        compiler_params=_cparams(("parallel", "parallel")),
        name="ffn",
    )(x, mod, ng, wup, wdn)


def _mix_kernel(x_ref, mod_ref, ng_ref, win_ref, halo_ref, convw_ref, mu_ref, w0_ref, ww2_ref,
                a0_ref, wa2_ref, wg2_ref, kk_ref, ka_ref, poolw_ref, pools_ref, qg_ref, kg_ref,
                u_ref, ya_ref, yc_ref, rw_ref, q_ref, k_ref, v_ref, kb_ref, vb_ref,
                conv_ref, shift_ref, pool_ref, ext, *, tm, t_real, pos0, nt):
    t = pl.program_id(1)

    @pl.when(t == 0)
    def _():
        ext[:, 0:HALO, :] = halo_ref[...]

    x = x_ref[...]
    bb = x.shape[0]
    rows = bb * tm
    u = _modulated_norm(x, mod_ref, ng_ref, 1)
    ub = u.reshape(rows, D_MODEL).astype(BF16)
    u_ref[...] = ub.reshape(bb, tm, D_MODEL)
    proj = jnp.dot(ub, win_ref[...], preferred_element_type=F32)

    bg = proj[:, 0:D_BR]
    z = proj[:, D_BR:2 * D_BR] * proj[:, 2 * D_BR:3 * D_BR]
    p_rwkv = proj[:, 3 * D_BR:3 * D_BR + D_MODEL]
    p_pool = proj[:, 3 * D_BR + D_MODEL:4 * D_BR + D_MODEL]
    ext[:, HALO:HALO + tm, 0:D_BR] = z.reshape(bb, tm, D_BR)
    ext[:, HALO:HALO + tm, D_BR:D_BR + D_MODEL] = p_rwkv.reshape(bb, tm, D_MODEL)
    ext[:, HALO:HALO + tm, D_BR + D_MODEL:EXT_W] = p_pool.reshape(bb, tm, D_BR)
    cw = convw_ref[...]
    z1 = ext[:, HALO - 1:HALO - 1 + tm, 0:D_BR].reshape(rows, D_BR)
    z2 = ext[:, HALO - 2:HALO - 2 + tm, 0:D_BR].reshape(rows, D_BR)
    ya = bg * (cw[0:1] * z2 + cw[1:2] * z1 + cw[2:3] * z)
    ya_ref[...] = ya.astype(BF16).reshape(bb, tm, D_BR)

    pp = ext[:, HALO - 1:HALO - 1 + tm, D_BR:D_BR + D_MODEL].reshape(rows, D_MODEL)
    xs = p_rwkv + (pp - p_rwkv) * mu_ref[...]
    r = xs[:, 0:D_BR]
    k = xs[:, D_BR:2 * D_BR]
    v = xs[:, 2 * D_BR:3 * D_BR]
    wd = xs[:, 3 * D_BR:3 * D_BR + 64]
    ad = xs[:, 3 * D_BR + 64:3 * D_BR + 128]
    gd = xs[:, 3 * D_BR + 128:D_MODEL]
    w_log = -_softplus(-(w0_ref[...] + _mm(jnp.tanh(wd), ww2_ref[...]))) - 0.5
    logw = -jnp.exp(w_log)
    a = _sigmoid(a0_ref[...] + _mm(ad, wa2_ref[...]))
    g = _mm(_sigmoid(gd), wg2_ref[...])
    hsum = _head_sum_matrix()
    kkr = k * kk_ref[...]
    kk = kkr / jnp.maximum(jnp.sqrt(_mm(kkr * kkr, hsum)), 1e-12)
    kmod = k * (1.0 + (a - 1.0) * ka_ref[...])
    bvec = kk * a
    tok = t * tm + (lax.broadcasted_iota(jnp.int32, (rows, 1), 0) & (tm - 1))
    if t_real < nt * tm:
        live = jnp.where(tok < t_real, 1.0, 0.0)
        logw, kmod, v, kk, bvec = logw * live, kmod * live, v * live, kk * live, bvec * live
    for i, val in enumerate((r, logw, kmod, v, kk, bvec, g)):
        rw_ref[:, :, i * D_BR:(i + 1) * D_BR] = val.reshape(bb, tm, D_BR)

    col = lax.broadcasted_iota(jnp.int32, (1, D_BR), 1)
    acc = p_pool
    sel = None
    for i in range(1, POOL_WINDOWS[-1]):
        acc = acc + ext[:, HALO - i:HALO - i + tm, D_BR + D_MODEL:EXT_W].reshape(rows, D_BR)
        if (i + 1) in POOL_WINDOWS:
            gi = POOL_WINDOWS.index(i + 1)
            sel = acc if sel is None else jnp.where(col >= gi * 64, acc, sel)
    wcol = jnp.where(col < 64, 2, jnp.where(col < 128, 4, jnp.where(col < 192, 8, 16)))
    cnt = jnp.minimum(wcol, pos0 + tok + 1).astype(F32)
    dd = sel / cnt - p_pool
    yc = _mm(dd, poolw_ref[...]) * pools_ref[...]
    yc_ref[...] = yc.astype(BF16).reshape(bb, tm, D_BR)

    q = proj[:, 4 * D_BR + D_MODEL:5 * D_BR + D_MODEL]
    ks = proj[:, 5 * D_BR + D_MODEL:6 * D_BR + D_MODEL]
    vs = proj[:, 6 * D_BR + D_MODEL:D_IN]
    qn = q * lax.rsqrt(_mm(q * q, hsum) * (1.0 / HEAD_DIM) + EPS) * qg_ref[...]
    kn = ks * lax.rsqrt(_mm(ks * ks, hsum) * (1.0 / HEAD_DIM) + EPS) * kg_ref[...]
    q_ref[...] = (qn * HEAD_DIM ** -0.5).astype(BF16).reshape(bb, tm, D_BR)
    k_ref[...] = kn.reshape(bb, tm, D_BR)
    v_ref[...] = vs.reshape(bb, tm, D_BR)
    kb_ref[...] = kn.astype(BF16).reshape(bb, tm, D_BR)
    vb_ref[...] = vs.astype(BF16).reshape(bb, tm, D_BR)

    @pl.when(t == nt - 1)
    def _():
        last = HALO + t_real - (nt - 1) * tm - 1
        conv_ref[...] = ext[:, last - (CONV_W - 2):last + 1, 0:D_BR]
        shift_ref[...] = ext[:, last:last + 1, D_BR:D_BR + D_MODEL]
        pool_ref[...] = ext[:, last - (POOL_CTX - 1):last + 1, D_BR + D_MODEL:EXT_W]

    if nt > 1:
        ext[:, 0:HALO, :] = ext[:, tm:tm + HALO, :]


def _mix_call(x, mod, ng, win, halo0, p, l, bb, tm, t_real, pos0):
    bsz, t, _ = x.shape
    nt = t // tm
    kern = functools.partial(_mix_kernel, tm=tm, t_real=t_real, pos0=pos0, nt=nt)
    tok = lambda w: pl.BlockSpec((bb, tm, w), lambda i, j: (i, j, 0))
    per_b = lambda r, w: pl.BlockSpec((bb, r, w), lambda i, j: (i, 0, 0))
    row = lambda r, w: pl.BlockSpec((None, r, w), lambda i, j: (l, 0, 0))
    sds = lambda shape, dt: jax.ShapeDtypeStruct(shape, dt)
    return pl.pallas_call(
        kern,
        grid=(bsz // bb, nt),
        in_specs=[tok(D_MODEL), per_b(N_MOD, D_MODEL),
                  pl.BlockSpec((None, 1, D_MODEL), lambda i, j: (3 * l + 1, 0, 0)),
                  _resident((None, D_MODEL, D_IN), lambda i, j: (l, 0, 0)),
                  per_b(HALO, EXT_W),
                  row(CONV_W, D_BR), row(1, D_MODEL), row(1, D_BR), row(64, D_BR), row(1, D_BR),
                  row(64, D_BR), row(128, D_BR), row(1, D_BR), row(1, D_BR), row(D_BR, D_BR),
                  row(1, D_BR), row(1, D_BR), row(1, D_BR)],
        out_specs=[tok(D_MODEL), tok(D_BR), tok(D_BR), tok(RW_W), tok(D_BR), tok(D_BR), tok(D_BR),
                   tok(D_BR), tok(D_BR),
                   per_b(CONV_W - 1, D_BR), per_b(1, D_MODEL), per_b(POOL_CTX, D_BR)],
        out_shape=[sds((bsz, t, D_MODEL), BF16), sds((bsz, t, D_BR), BF16), sds((bsz, t, D_BR), BF16),
                   sds((bsz, t, RW_W), F32), sds((bsz, t, D_BR), BF16), sds((bsz, t, D_BR), F32),
                   sds((bsz, t, D_BR), F32), sds((bsz, t, D_BR), BF16), sds((bsz, t, D_BR), BF16),
                   sds((bsz, CONV_W - 1, D_BR), F32),
                   sds((bsz, 1, D_MODEL), F32), sds((bsz, POOL_CTX, D_BR), F32)],
        scratch_shapes=[pltpu.VMEM((bb, HALO + tm, EXT_W), F32)],
        compiler_params=_cparams(("parallel", "arbitrary")),
        name="mix_prep",
    )(x, mod, ng, win, halo0, p["conv_w"], p["mu"], p["w0"], p["w2"], p["a0"], p["a2"], p["g2"],
      p["k_k"], p["k_a"], p["pool_bd"], p["pool_scale"], p["q_g"], p["k_g"])


def _rwkv_kernel(rw_ref, s0_ref, rk_ref, lng_ref, lnb_ref, y_ref, sout_ref, s_scr, *, nc, bb):
    c_idx = pl.program_id(1)

    @pl.when(c_idx == 0)
    def _():
        s_scr[...] = s0_ref[...]

    c_in = rw_ref.shape[1]
    csz = RWKV_CHUNK
    ri = lax.broadcasted_iota(jnp.int32, (csz, csz), 0)
    ci = lax.broadcasted_iota(jnp.int32, (csz, csz), 1)
    lower = ri >= ci
    strict = ri > ci
    tril = jnp.where(lower, 1.0, 0.0).astype(BF16)
    eye = jnp.where(ri == ci, 1.0, 0.0)
    levels = [((ri >> (s + 1)) == (ci >> (s + 1))) & ((ri >> s) != (ci >> s))
              for s in range(csz.bit_length() - 1)]

    probs = [(b, h) for b in range(bb) for h in range(N_HEADS)]
    each = lambda fn, *lists: [fn(*args) for args in zip(*lists)]

    rws, cums = [], []
    for b in range(bb):
        rw = rw_ref[b]
        if c_in < csz:
            rw = jnp.concatenate([rw, jnp.zeros((csz - c_in, RW_W), F32)], axis=0)
        lh, ll = _split(rw[:, D_BR:2 * D_BR])
        cums.append(jnp.dot(tril, lh, preferred_element_type=F32)
                    + jnp.dot(tril, ll, preferred_element_type=F32))
        rws.append(rw)

    def operands(b, h):
        lo = h * HEAD_DIM
        r, logw, k, v, kk, bv, g = (rws[b][:, i * D_BR + lo:i * D_BR + lo + HEAD_DIM] for i in range(7))
        cum = cums[b][:, lo:lo + HEAD_DIM]
        tot = cum[csz - 1:csz, :]
        e_inv = jnp.exp(-cum)
        e_rest = jnp.exp(tot - cum)
        return dict(r=r, k=k, v=v, g=g, kt=kk * jnp.exp(cum - logw), bt=bv * e_inv, kd=k * e_inv,
                    rt=r * jnp.exp(cum), bh=bv * e_rest, kh=k * e_rest, gam=jnp.exp(tot))

    ops = [operands(b, h) for b, h in probs]
    get = lambda name: [o[name] for o in ops]
    kt, bt, kd, rt, v = get("kt"), get("bt"), get("kd"), get("rt"), get("v")

    kr = each(lambda a, c: jnp.concatenate([a, c], axis=0), kt, rt)
    pb = each(_mm_nt, kr, bt)
    pk = each(_mm_nt, kr, kd)
    a_b = [jnp.where(strict, p[0:csz], 0.0) for p in pb]
    m_b = [jnp.where(lower, p[csz:2 * csz], 0.0) for p in pb]
    a_k = [jnp.where(strict, p[0:csz], 0.0) for p in pk]
    m_k = [jnp.where(lower, p[csz:2 * csz], 0.0) for p in pk]

    tinv = [eye - jnp.where(levels[0], a, 0.0) for a in a_b]
    for lv in levels[1:]:
        x = each(lambda a, t: _mm(jnp.where(lv, a, 0.0), t), a_b, tinv)
        tinv = each(lambda t, xx: t - _mm(t, xx), tinv, x)

    akv = each(_mm, a_k, v)
    w1 = each(_mm, tinv, kt)
    w2 = each(_mm, tinv, akv)
    qt = each(lambda r_, m, w: r_ - _mm(m, w), rt, m_b, w1)
    y0 = each(lambda mk, vv, mb, w: _mm(mk, vv) - _mm(mb, w), m_k, v, m_b, w2)
    m_neg = each(_mm_tn, w1, get("bh"))
    n_add = each(lambda vv, w, kh_, bh_: _mm_tn(jnp.concatenate([vv, w], axis=0),
                                                jnp.concatenate([kh_, -bh_], axis=0)),
                 v, w2, get("kh"), get("bh"))

    s_prev = [s_scr[b, h] for b, h in probs]
    y = each(lambda q_, s_, y_: _mm_nt(q_, s_) + y_, qt, s_prev, y0)
    s_new = each(lambda s_, o, mn, na: s_ * o["gam"] - _mm(s_, mn) + na, s_prev, ops, m_neg, n_add)
    for (b, h), s_ in zip(probs, s_new):
        s_scr[b, h] = s_

    for (b, h), o, y_ in zip(probs, ops, y):
        sl = slice(h * HEAD_DIM, (h + 1) * HEAD_DIM)
        mean = jnp.mean(y_, axis=-1, keepdims=True)
        yc = y_ - mean
        var = jnp.mean(yc * yc, axis=-1, keepdims=True)
        yn = yc * lax.rsqrt(var + LNX_EPS) * lng_ref[:, sl] + lnb_ref[:, sl]
        bonus = jnp.sum(o["r"] * o["k"] * rk_ref[:, sl], axis=-1, keepdims=True) * o["v"]
        y_ref[b, :, sl] = ((yn + bonus) * o["g"])[0:c_in].astype(BF16)

    @pl.when(c_idx == nc - 1)
    def _():
        sout_ref[...] = s_scr[...]


def _rwkv_call(rw, s0, p, l, csz, bb):
    bsz, t, _ = rw.shape
    nc = t // csz
    row = lambda: pl.BlockSpec((None, 1, D_BR), lambda i, j: (l, 0, 0))
    st = pl.BlockSpec((bb, N_HEADS, HEAD_DIM, HEAD_DIM), lambda i, j: (i, 0, 0, 0))
    return pl.pallas_call(
        functools.partial(_rwkv_kernel, nc=nc, bb=bb),
        grid=(bsz // bb, nc),
        in_specs=[pl.BlockSpec((bb, csz, RW_W), lambda i, j: (i, j, 0)), st, row(), row(), row()],
        out_specs=[pl.BlockSpec((bb, csz, D_BR), lambda i, j: (i, j, 0)), st],
        out_shape=[jax.ShapeDtypeStruct((bsz, t, D_BR), BF16),
                   jax.ShapeDtypeStruct((bsz, N_HEADS, HEAD_DIM, HEAD_DIM), F32)],
        scratch_shapes=[pltpu.VMEM((bb, N_HEADS, HEAD_DIM, HEAD_DIM), F32)],
        compiler_params=_cparams(("parallel", "arbitrary")),
        name="rwkv",
    )(rw, s0, p["r_k"], p["lnx_g"], p["lnx_b"])


def _sb_block(z, causal, tri, carry):
    ls, lk = _log_sigmoids(z)
    if causal is not None:
        lk = jnp.where(causal, lk, 0.0)
    after = jnp.dot(lk.astype(BF16), tri, preferred_element_type=F32)
    a = jnp.exp(ls + after + carry)
    if causal is not None:
        a = jnp.where(causal, a, 0.0)
    return a, carry + after[:, 0:1] + lk[:, 0:1]


def _sba_prompt_kernel(qi_ref, kb_ref, first_ref, last_ref, bias_ref, q_ref, k_ref, v_ref, tri_ref,
                       o_ref, acc, carry, *, tq):
    p = pl.program_id(1)
    first = first_ref[p] == 1
    half = tri_ref.shape[0]

    @pl.when(first)
    def _():
        acc[...] = jnp.zeros_like(acc)
        carry[...] = jnp.zeros_like(carry)

    def compute(diag):
        q = q_ref[0]
        kb = k_ref[0]
        vb = v_ref[0]
        tri = tri_ref[...]
        causal = None
        if diag:
            causal = (lax.broadcasted_iota(jnp.int32, (tq, tq), 1)
                      < lax.broadcasted_iota(jnp.int32, (tq, tq), 0))
        for h in range(N_HEADS):
            sl = slice(h * HEAD_DIM, (h + 1) * HEAD_DIM)
            out = None
            c = carry[h]
            for lo in range(tq - half, -1, -half):
                r0 = lo if diag else 0
                z = lax.dot_general(q[r0:, sl], kb[lo:lo + half, sl], (((1,), (1,)), ((), ())),
                                    preferred_element_type=F32) + bias_ref[h]
                ls, lk = _log_sigmoids(z)
                if diag:
                    seen = causal[r0:, lo:lo + half]
                    lk = jnp.where(seen, lk, 0.0)
                inner = jnp.dot(lk.astype(BF16), tri, preferred_element_type=F32)
                a = jnp.exp(ls + inner)
                if diag:
                    a = jnp.where(seen, a, 0.0)
                part = jnp.exp(c[r0:]) * jnp.dot(a.astype(BF16), vb[lo:lo + half, sl],
                                                 preferred_element_type=F32)
                total = inner[:, 0:1] + lk[:, 0:1]
                if r0:
                    part = jnp.concatenate([jnp.zeros((r0, HEAD_DIM), F32), part], axis=0)
                    total = jnp.concatenate([jnp.zeros((r0, 1), F32), total], axis=0)
                out = part if out is None else out + part
                c = c + total
            carry[h] = c
            acc[:, sl] += out

    @pl.when(first)
    def _():
        compute(True)

    @pl.when(jnp.logical_not(first))
    def _():
        compute(False)

    @pl.when(last_ref[p] == 1)
    def _():
        o_ref[0] = acc[...].astype(BF16)


def _sba_prompt_call(q, k, v, bias, tri, tq):
    bsz, t, _ = q.shape
    nq = t // tq
    pairs = [(i, i - j, int(j == 0), int(j == i)) for i in range(nq) for j in range(i + 1)]
    qi_tbl, kb_tbl, first_tbl, last_tbl = (jnp.asarray(c, jnp.int32) for c in zip(*pairs))
    q_map = lambda b, p, qi, kb, fi, la: (b, qi[p], 0)
    kv_map = lambda b, p, qi, kb, fi, la: (b, kb[p], 0)
    grid_spec = pltpu.PrefetchScalarGridSpec(
        num_scalar_prefetch=4,
        grid=(bsz, len(pairs)),
        in_specs=[pl.BlockSpec(memory_space=pltpu.SMEM),
                  pl.BlockSpec((1, tq, D_BR), q_map),
                  pl.BlockSpec((1, tq, D_BR), kv_map),
                  pl.BlockSpec((1, tq, D_BR), kv_map),
                  pl.BlockSpec(tri.shape, lambda b, p, qi, kb, fi, la: (0, 0))],
        out_specs=pl.BlockSpec((1, tq, D_BR), q_map),
        scratch_shapes=[pltpu.VMEM((tq, D_BR), F32), pltpu.VMEM((N_HEADS, tq, 1), F32)],
    )
    return pl.pallas_call(
        functools.partial(_sba_prompt_kernel, tq=tq),
        grid_spec=grid_spec,
        out_shape=jax.ShapeDtypeStruct((bsz, t, D_BR), BF16),
        compiler_params=_cparams(("parallel", "arbitrary")),
        name="sba_prompt",
    )(qi_tbl, kb_tbl, first_tbl, last_tbl, bias, q, k, v, tri)


def _sba_sample_kernel(pt_ref, bias_ref, q_ref, kn_ref, vn_ref, tri_ref, ck_hbm, cv_hbm, o_ref,
                       kbuf, vbuf, sems, acc, carry, *, layer, npg, tpad, nsteps, n_pages, total):
    g = pl.program_id(0)
    s = g & (nsteps - 1)
    slot = g & 1
    rows = N_HEADS * tpad

    def page_copy(step, i, src_hbm, buf, which, page):
        return pltpu.make_async_copy(src_hbm.at[layer, page], buf.at[step & 1, i], sems.at[step & 1, which])

    def start_step(step):
        seq = step >> (nsteps.bit_length() - 1)
        first = n_pages - ((step & (nsteps - 1)) + 1) * npg
        for i in range(npg):
            page = pt_ref[seq, first + i]
            page_copy(step, i, ck_hbm, kbuf, 0, page).start()
            page_copy(step, i, cv_hbm, vbuf, 1, page).start()

    @pl.when(g == 0)
    def _():
        start_step(g)

    @pl.when(g + 1 < total)
    def _():
        start_step(g + 1)

    tri = tri_ref[...]
    tshift = tpad.bit_length() - 1
    rid = lax.broadcasted_iota(jnp.int32, (rows, D_BR), 0) >> tshift
    cid = lax.broadcasted_iota(jnp.int32, (rows, D_BR), 1) >> 6
    head_match = rid == cid
    q = q_ref[0].astype(F32)
    q_bd = jnp.where(head_match, jnp.concatenate([q] * N_HEADS, axis=0), 0.0).astype(BF16)
    hrow = lax.broadcasted_iota(jnp.int32, (rows, 1), 0) >> tshift
    bias = jnp.zeros((rows, 1), F32)
    for h in range(N_HEADS):
        bias = jnp.where(hrow == h, bias_ref[h], bias)

    @pl.when(s == 0)
    def _():
        kpos = lax.broadcasted_iota(jnp.int32, (rows, PAGE), 1)
        qpos = lax.broadcasted_iota(jnp.int32, (rows, PAGE), 0) & (tpad - 1)
        z = lax.dot_general(q_bd, kn_ref[0].astype(BF16), (((1,), (1,)), ((), ())),
                            preferred_element_type=F32) + bias
        a, c_new = _sb_block(z, kpos < qpos, tri, jnp.zeros((rows, 1), F32))
        carry[...] = c_new
        acc[...] = jnp.dot(a.astype(BF16), vn_ref[0].astype(BF16), preferred_element_type=F32)

    for i in range(npg):
        page_copy(g, i, ck_hbm, kbuf, 0, 0).wait()
        page_copy(g, i, cv_hbm, vbuf, 1, 0).wait()
    kt = jnp.concatenate([kbuf[slot, i] for i in range(npg)], axis=1).astype(BF16)
    vt = jnp.concatenate([vbuf[slot, i] for i in range(npg)], axis=1).astype(BF16)
    z = jnp.dot(q_bd, kt, preferred_element_type=F32) + bias
    ls, lk = _log_sigmoids(z)
    page = lambda x, i: x[:, i * PAGE:(i + 1) * PAGE]
    local = [jnp.dot(page(lk, i).astype(BF16), tri, preferred_element_type=F32) for i in range(npg)]
    totals = [local[i][:, 0:1] + page(lk, i)[:, 0:1] for i in range(npg)]
    c = carry[...]
    after = [None] * npg
    for i in range(npg - 1, -1, -1):
        after[i] = local[i] + c
        c = c + totals[i]
    carry[...] = c
    a = jnp.exp(ls + jnp.concatenate(after, axis=1))
    acc[...] += lax.dot_general(a.astype(BF16), vt, (((1,), (1,)), ((), ())), preferred_element_type=F32)

    @pl.when(s == nsteps - 1)
    def _():
        masked = jnp.where(head_match, acc[...], 0.0)
        out = masked[0:tpad]
        for h in range(1, N_HEADS):
            out = out + masked[h * tpad:(h + 1) * tpad]
        o_ref[0] = out.astype(BF16)


def _sba_sample_call(q, kn, vn, cache_k, cache_v, page_table, bias, tri, l, npg):
    bsz, tpad, _ = q.shape
    n_pages = page_table.shape[1]
    nsteps = n_pages // npg
    rows = N_HEADS * tpad
    assert nsteps & (nsteps - 1) == 0 and nsteps * npg == n_pages
    per_seq = lambda r: pl.BlockSpec((1, r, D_BR), lambda g, pt: (g // nsteps, 0, 0))
    grid_spec = pltpu.PrefetchScalarGridSpec(
        num_scalar_prefetch=1,
        grid=(bsz * nsteps,),
        in_specs=[pl.BlockSpec(memory_space=pltpu.SMEM),
                  per_seq(tpad), per_seq(PAGE), per_seq(PAGE),
                  pl.BlockSpec((PAGE, PAGE), lambda g, pt: (0, 0)),
                  pl.BlockSpec(memory_space=pl.ANY), pl.BlockSpec(memory_space=pl.ANY)],
        out_specs=per_seq(tpad),
        scratch_shapes=[pltpu.VMEM((2, npg, D_BR, PAGE), F32), pltpu.VMEM((2, npg, D_BR, PAGE), F32),
                        pltpu.SemaphoreType.DMA((2, 2)),
                        pltpu.VMEM((rows, D_BR), F32), pltpu.VMEM((rows, 1), F32)],
    )
    return pl.pallas_call(
        functools.partial(_sba_sample_kernel, layer=l, npg=npg, tpad=tpad, nsteps=nsteps,
                          n_pages=n_pages, total=bsz * nsteps),
        grid_spec=grid_spec,
        out_shape=jax.ShapeDtypeStruct((bsz, tpad, D_BR), BF16),
        compiler_params=_cparams(("arbitrary",)),
        name="sba_sample",
    )(page_table, bias, q, kn, vn, tri, cache_k, cache_v)


def _merge_kernel(x_ref, mod_ref, u_ref, ya_ref, yb_ref, yc_ref, yd_ref, wg_ref, wb_ref, wo_ref, o_ref):
    x = x_ref[...]
    bb, tm, _ = x.shape
    rows = bb * tm
    ub = u_ref[...].reshape(rows, D_MODEL)
    ys = [r[...].reshape(rows, D_BR) for r in (ya_ref, yb_ref, yc_ref, yd_ref)]
    out = jnp.zeros((rows, D_MODEL), F32)
    for c in range(D_MODEL // D_BR):
        cols = slice(c * D_BR, (c + 1) * D_BR)
        merged = jnp.zeros((rows, D_BR), F32)
        for i, y in enumerate(ys):
            gate = _sigmoid(jnp.dot(ub, wg_ref[i, :, cols], preferred_element_type=F32))
            merged = merged + gate * jnp.dot(y, wb_ref[i, :, cols], preferred_element_type=F32)
        out = out + jnp.dot(merged.astype(BF16), wo_ref[cols, :], preferred_element_type=F32)
    g2 = mod_ref[:, 5:6, :]
    o_ref[...] = x + g2 * out.reshape(bb, tm, D_MODEL)


def _merge_call(x, mod, ub, ya, yb, yc, yd, wg, wb, wo, l, bb, tm):
    bsz, t, _ = x.shape
    tok = lambda w: pl.BlockSpec((bb, tm, w), lambda i, j: (i, j, 0))
    return pl.pallas_call(
        _merge_kernel,
        grid=(bsz // bb, t // tm),
        in_specs=[tok(D_MODEL), pl.BlockSpec((bb, N_MOD, D_MODEL), lambda i, j: (i, 0, 0)),
                  tok(D_MODEL), tok(D_BR), tok(D_BR), tok(D_BR), tok(D_BR),
                  _resident((None, N_BRANCH, D_MODEL, D_MODEL), lambda i, j: (l, 0, 0, 0)),
                  _resident((None, N_BRANCH, D_BR, D_MODEL), lambda i, j: (l, 0, 0, 0)),
                  _resident((None, D_MODEL, D_MODEL), lambda i, j: (l, 0, 0))],
        out_specs=tok(D_MODEL),
        out_shape=jax.ShapeDtypeStruct(x.shape, F32),
        compiler_params=_cparams(("parallel", "parallel")),
        name="merge",
    )(x, mod, ub, ya, yb, yc, yd, wg, wb, wo)


def _tri_matrix(n):
    r = lax.broadcasted_iota(jnp.int32, (n, n), 0)
    c = lax.broadcasted_iota(jnp.int32, (n, n), 1)
    return jnp.where(r > c, 1.0, 0.0).astype(BF16)


def kernel(x_prompt, x_sample, c_prompt, c_sample, cache_k, cache_v, page_table, state_conv, state_shift, state_wkv, state_pool, ada_w, ada_b, norm_g, w_ffn_up, w_ffn_down, w_in, conv_w, rwkv_mu, rwkv_w0, rwkv_w2, rwkv_a0, rwkv_a2, rwkv_g2, rwkv_k_k, rwkv_k_a, rwkv_r_k, rwkv_lnx_g, rwkv_lnx_b, pool_w, pool_scale, q_norm_g, k_norm_g, sb_bias, w_gate, w_branch, w_out):
    depth = ada_w.shape[0]
    bp, seq, _ = x_prompt.shape
    bs, dec_seq, _ = x_sample.shape
    n_phys = cache_k.shape[1]
    past_len = page_table.shape[1] * PAGE
    tpad = 8
    tm_p = 512
    tq_p = 1024 if seq % 1024 == 0 else 512

    wup = w_ffn_up.astype(BF16)
    wdn = w_ffn_down.astype(BF16)
    win = w_in.astype(BF16)
    wg = w_gate.astype(BF16)
    wb = w_branch.astype(BF16)
    wo = w_out.astype(BF16)
    ng = norm_g.reshape(depth * 3, 1, D_MODEL)
    pool_bd = jnp.zeros((depth, len(POOL_WINDOWS), 64, len(POOL_WINDOWS), 64), F32)
    for gi in range(len(POOL_WINDOWS)):
        pool_bd = pool_bd.at[:, gi, :, gi, :].set(pool_w[:, gi])
    prm = {
        "conv_w": conv_w, "mu": rwkv_mu[:, None, :], "w0": rwkv_w0[:, None, :], "w2": rwkv_w2,
        "a0": rwkv_a0[:, None, :], "a2": rwkv_a2, "g2": rwkv_g2, "k_k": rwkv_k_k[:, None, :],
        "k_a": rwkv_k_a[:, None, :], "pool_bd": pool_bd.reshape(depth, D_BR, D_BR).astype(BF16),
        "pool_scale": pool_scale[:, None, :],
        "q_g": jnp.tile(q_norm_g, (1, N_HEADS))[:, None, :],
        "k_g": jnp.tile(k_norm_g, (1, N_HEADS))[:, None, :],
        "r_k": rwkv_r_k[:, None, :], "lnx_g": rwkv_lnx_g[:, None, :], "lnx_b": rwkv_lnx_b[:, None, :],
    }
    ck = jnp.transpose(cache_k, (0, 1, 3, 4, 2)).reshape(depth, n_phys, D_BR, PAGE)
    cv = jnp.transpose(cache_v, (0, 1, 3, 4, 2)).reshape(depth, n_phys, D_BR, PAGE)
    tri_p = _tri_matrix(256)
    tri_s = _tri_matrix(PAGE)

    n_c = bp + bs
    c_all = jnp.concatenate([c_prompt, c_sample, jnp.zeros((-n_c % 8, D_MODEL), F32)], axis=0)
    mod_all = _ada_call(c_all, ada_w, ada_b)
    xs_pad = jnp.concatenate([x_sample, jnp.zeros((bs, tpad - dec_seq, D_MODEL), F32)], axis=1)

    halo_p = jnp.zeros((bp, HALO, EXT_W), F32)
    wkv0_p = jnp.zeros((bp, N_HEADS, HEAD_DIM, HEAD_DIM), F32)

    hp, hs = x_prompt, xs_pad
    outs_p = [[] for _ in range(6)]
    outs_s = [[] for _ in range(6)]
    for l in range(depth):
        mod_p = mod_all[l, :bp].reshape(bp, N_MOD, D_MODEL)
        mod_s = mod_all[l, bp:n_c].reshape(bs, N_MOD, D_MODEL)
        halo_s = jnp.zeros((bs, HALO, EXT_W), F32)
        halo_s = halo_s.at[:, HALO - (CONV_W - 1):, 0:D_BR].set(state_conv[l])
        halo_s = halo_s.at[:, HALO - 1, D_BR:D_BR + D_MODEL].set(state_shift[l])
        halo_s = halo_s.at[:, HALO - POOL_CTX:, D_BR + D_MODEL:].set(state_pool[l])

        for grp in ("p", "s"):
            if grp == "p":
                x, mod, bb, tm, t_real, pos0, halo0, wkv0 = hp, mod_p, 1, tm_p, seq, 0, halo_p, wkv0_p
            else:
                x, mod, bb, tm, t_real, pos0, halo0, wkv0 = hs, mod_s, bs, tpad, dec_seq, past_len, halo_s, state_wkv[l]
            ffn_tile = (2 * tm, 11) if (grp == "p" and seq % (2 * tm) == 0) else (tm, 2)
            x = _ffn_call(x, mod, ng, wup, wdn, l, 0, bb, *ffn_tile)
            ub, ya, yc, rw, q, k, v, k16, v16, conv_n, shift_n, pool_n = _mix_call(
                x, mod, ng, win, halo0, prm, l, bb, tm, t_real, pos0)
            seqs = max(d for d in (1, 2, 4) if rw.shape[0] % d == 0)
            yb, wkv_n = _rwkv_call(rw, wkv0, prm, l, RWKV_CHUNK if grp == "p" else tpad, seqs)
            if grp == "p":
                yd = _sba_prompt_call(q, k16, v16, sb_bias[l], tri_p, tq_p)
            else:
                kn = jnp.concatenate([k, jnp.zeros((bs, PAGE - tpad, D_BR), F32)], axis=1)
                vn = jnp.concatenate([v, jnp.zeros((bs, PAGE - tpad, D_BR), F32)], axis=1)
                yd = _sba_sample_call(q, kn, vn, ck, cv, page_table, sb_bias[l], tri_s, l,
                                      16 if page_table.shape[1] % 16 == 0 else 8)
            x = _merge_call(x, mod, ub, ya, yb, yc, yd, wg, wb, wo, l, bb, ffn_tile[0])
            x = _ffn_call(x, mod, ng, wup, wdn, l, 1, bb, *ffn_tile)
            new = (k[:, :t_real].reshape(-1, t_real, N_HEADS, HEAD_DIM),
                   v[:, :t_real].reshape(-1, t_real, N_HEADS, HEAD_DIM),
                   conv_n, shift_n[:, 0], wkv_n, pool_n)
            if grp == "p":
                hp = x
                for lst, a in zip(outs_p, new):
                    lst.append(a)
            else:
                hs = x
                for lst, a in zip(outs_s, new):
                    lst.append(a)

    stacked_p = [jnp.stack(a, axis=0) for a in outs_p]
    stacked_s = [jnp.stack(a, axis=0) for a in outs_s]
    return (hp, hs[:, :dec_seq], *stacked_p, *stacked_s)
```
